```python
import math
import jax, jax.numpy as jnp
from jax import lax
import numpy as np

D_MODEL = 1024
BATCH = 8
SEQ = 4096
DEPTH = 1

SWA_HEAD_DIM = 64
SWA_HEADS = (D_MODEL // 2) // SWA_HEAD_DIM
SWA_WIDTH = SWA_HEADS * SWA_HEAD_DIM
SWA_PATTERNS = ((128, 1), (512, 4), (2048, 16))
SWA_ROT_DIM = SWA_HEAD_DIM // 4
MLA_NOPE_DIM = 128
MLA_ROPE_DIM = 64
MLA_V_DIM = 128
MLA_QK_DIM = MLA_NOPE_DIM + MLA_ROPE_DIM
MLA_HEADS = (D_MODEL // 2) // MLA_V_DIM
MLA_WIDTH = MLA_HEADS * MLA_V_DIM
MLA_Q_RANK = D_MODEL // 4
MLA_KV_RANK = D_MODEL // 8
Q_BLOCK = 128
MIX_WIDTH = SWA_WIDTH + MLA_WIDTH
IN_SPLITS = (SWA_WIDTH, 2 * SWA_WIDTH, 3 * SWA_WIDTH,
             3 * SWA_WIDTH + MLA_Q_RANK,
             3 * SWA_WIDTH + MLA_Q_RANK + MLA_KV_RANK)
IN_WIDTH = 3 * SWA_WIDTH + MLA_Q_RANK + MLA_KV_RANK + MLA_ROPE_DIM
D_FF = 2816
CONV_WIDTH = 3
ROPE_THETA = 500000.0
LN_EPS = 1e-5
RMS_EPS = 1e-6
NEG_INF = -1e30
DEEPNORM_ALPHA = (2.0 * DEPTH) ** 0.25
DEEPNORM_BETA = (8.0 * DEPTH) ** -0.25

kernel_name = 'hybrid_dilated_swa_mla_convffn_deepnorm'


def layer_norm(x, g, b):
    xf = x.astype(jnp.float32)
    mu = jnp.mean(xf, axis=-1, keepdims=True)
    xc = xf - mu
    var = jnp.mean(xc * xc, axis=-1, keepdims=True)
    y = xc * lax.rsqrt(var + LN_EPS) * g.astype(jnp.float32) + b.astype(jnp.float32)
    return y.astype(x.dtype)


def rms_norm(x, g, out_dtype):
    xf = x.astype(jnp.float32)
    y = xf * lax.rsqrt(jnp.mean(xf * xf, axis=-1, keepdims=True) + RMS_EPS) * g.astype(jnp.float32)
    return y.astype(out_dtype)


def rope(x, positions, rot_dim):
    half = rot_dim // 2
    inv_freq = jnp.power(jnp.float32(ROPE_THETA),
                         -jnp.arange(half, dtype=jnp.float32) * (2.0 / rot_dim))
    ang = positions.astype(jnp.float32)[:, :, None] * inv_freq
    cos = jnp.cos(ang)[:, :, None, :]
    sin = jnp.sin(ang)[:, :, None, :]
    xr = x[..., :rot_dim].astype(jnp.float32)
    x1, x2 = xr[..., :half], xr[..., half:]
    rot = jnp.concatenate([x1 * cos - x2 * sin, x2 * cos + x1 * sin], axis=-1).astype(x.dtype)
    return jnp.concatenate([rot, x[..., rot_dim:]], axis=-1)


def dilated_band_attention(q, k, v, dilation, n_side):
    B, S, H, Dh = q.shape
    L = S // dilation
    blk = math.gcd(L, n_side)
    nblk = L // blk
    span = blk + 2 * n_side

    def to_residue(t):
        return t.reshape(B, L, dilation, H, Dh).transpose(0, 2, 1, 3, 4).reshape(B * dilation, L, H, Dh)

    def from_residue(t):
        tail = t.shape[3:]
        t = t.reshape((B, dilation, L) + tail)
        t = t.transpose((0, 2, 1) + tuple(range(3, t.ndim)))
        return t.reshape((B, S) + tail)

    qr = to_residue(q).reshape(B * dilation, nblk, blk, H, Dh).astype(jnp.float32)
    pad = ((0, 0), (n_side, n_side), (0, 0), (0, 0))
    kp = jnp.pad(to_residue(k), pad)
    vp = jnp.pad(to_residue(v), pad)
    key_idx = jnp.arange(nblk)[:, None] * blk + jnp.arange(span)[None, :]
    kb = kp[:, key_idx].astype(jnp.float32)
    vb = vp[:, key_idx].astype(jnp.float32)
    q_pos = jnp.arange(nblk)[:, None] * blk + jnp.arange(blk)[None, :]
    k_pos = key_idx - n_side
    valid = ((jnp.abs(q_pos[:, :, None] - k_pos[:, None, :]) <= n_side)
             & (k_pos[:, None, :] >= 0) & (k_pos[:, None, :] < L))

    s = jnp.einsum('znqhd,znkhd->znhqk', qr, kb) * (Dh ** -0.5)
    s = jnp.where(valid[None, :, None, :, :], s, NEG_INF)
    m = jnp.max(s, axis=-1, keepdims=True)
    p = jnp.exp(s - m)
    den = jnp.sum(p, axis=-1)
    o = jnp.einsum('znhqk,znkhd->znqhd', p, vb)
    o = o / den.transpose(0, 1, 3, 2)[..., None]
    lse = (m[..., 0] + jnp.log(den)).transpose(0, 1, 3, 2)
    return from_residue(o), from_residue(lse)


def dilated_mixture_attention(q, k, v):
    outs, lses = [], []
    for window, dilation in SWA_PATTERNS:
        o, lse = dilated_band_attention(q, k, v, dilation, window // (2 * dilation))
        outs.append(o)
        lses.append(lse)
    w = jax.nn.softmax(jnp.stack(lses, axis=0), axis=0)
    return jnp.einsum('pbsh,pbshd->bshd', w, jnp.stack(outs, axis=0))


def mla_attention(q, k, v):
    B, S, H, Dq = q.shape
    nb = S // Q_BLOCK
    qb = q.reshape(B, nb, Q_BLOCK, H, Dq).transpose(1, 0, 2, 3, 4)
    kf = k.astype(jnp.float32)
    vf = v.astype(jnp.float32)
    scale = Dq ** -0.5

    def block(qi):
        s = jnp.einsum('bqhd,bkhd->bhqk', qi.astype(jnp.float32), kf) * scale
        p = jax.nn.softmax(s, axis=-1)
        return jnp.einsum('bhqk,bkhd->bqhd', p, vf)

    o = lax.map(block, qb)
    return o.transpose(1, 0, 2, 3, 4).reshape(B, S, H, v.shape[-1])


def hybrid_layer(x, positions, w_in, q_norm_g, w_uq, kv_norm_g, w_ukv, out_norm_g, w_o,
                 ln1_g, ln1_b, w_up, conv_w, conv_b, w_down, ln2_g, ln2_b):
    B, S, _ = x.shape
    dt = x.dtype
    h = x @ w_in
    q_a, k_a, v_a, c_q, c_kv, k_rope = jnp.split(h, list(IN_SPLITS), axis=-1)

    q_a = rope(q_a.reshape(B, S, SWA_HEADS, SWA_HEAD_DIM), positions, SWA_ROT_DIM)
    k_a = rope(k_a.reshape(B, S, SWA_HEADS, SWA_HEAD_DIM), positions, SWA_ROT_DIM)
    v_a = v_a.reshape(B, S, SWA_HEADS, SWA_HEAD_DIM)
    o_a = dilated_mixture_attention(q_a, k_a, v_a).reshape(B, S, SWA_WIDTH)

    q_b = (rms_norm(c_q, q_norm_g, dt) @ w_uq).reshape(B, S, MLA_HEADS, MLA_QK_DIM)
    q_nope, q_pe = q_b[..., :MLA_NOPE_DIM], q_b[..., MLA_NOPE_DIM:]
    q_pe = rope(q_pe, positions, MLA_ROPE_DIM)
    kv = (rms_norm(c_kv, kv_norm_g, dt) @ w_ukv).reshape(B, S, MLA_HEADS, MLA_NOPE_DIM + MLA_V_DIM)
    k_nope, v_b = kv[..., :MLA_NOPE_DIM], kv[..., MLA_NOPE_DIM:]
    k_pe = rope(k_rope[:, :, None, :], positions, MLA_ROPE_DIM)
    k_pe = jnp.broadcast_to(k_pe, (B, S, MLA_HEADS, MLA_ROPE_DIM))
    q_full = jnp.concatenate([q_nope, q_pe], axis=-1)
    k_full = jnp.concatenate([k_nope, k_pe], axis=-1)
    o_b = mla_attention(q_full, k_full, v_b).reshape(B, S, MLA_WIDTH)

    o = jnp.concatenate([rms_norm(o_a, out_norm_g[:SWA_WIDTH], dt),
                         rms_norm(o_b, out_norm_g[SWA_WIDTH:], dt)], axis=-1)
    x = layer_norm(DEEPNORM_ALPHA * x + o @ w_o, ln1_g, ln1_b)

    u = x @ w_up
    half = CONV_WIDTH // 2
    up = jnp.pad(u, ((0, 0), (half, half), (0, 0)))
    uc = conv_b
    for t in range(CONV_WIDTH):
        uc = uc + up[:, t:t + S] * conv_w[t]
    gate, val = uc[..., :D_FF], uc[..., D_FF:]
    y = (jax.nn.silu(gate) * val) @ w_down
    x = layer_norm(DEEPNORM_ALPHA * x + y, ln2_g, ln2_b)
    return x


def setup_inputs(seed: int = 0) -> dict:
    key = jax.random.key(seed)
    ks = jax.random.split(key, 24)
    f32 = jnp.float32
    beta = DEEPNORM_BETA

    def nrm(k, shape, scale):
        return jax.random.normal(k, shape, f32) * scale

    def gain(k, n):
        return 1.0 + 0.02 * jax.random.normal(k, (DEPTH, n), f32)

    x = jax.random.normal(ks[0], (BATCH, SEQ, D_MODEL), f32)
    offsets = jax.random.randint(ks[1], (BATCH, 1), 0, 1024, dtype=jnp.int32)
    positions = (jnp.arange(SEQ, dtype=jnp.int32)[None, :] + offsets).astype(jnp.int32)

    ln_emb_g = 1.0 + 0.02 * jax.random.normal(ks[2], (D_MODEL,), f32)
    ln_emb_b = 0.02 * jax.random.normal(ks[3], (D_MODEL,), f32)

    in_col_scale = jnp.concatenate([
        jnp.ones((2 * SWA_WIDTH,), f32), jnp.full((SWA_WIDTH,), beta, f32),
        jnp.ones((MLA_Q_RANK + MLA_KV_RANK + MLA_ROPE_DIM,), f32)])
    w_in = nrm(ks[4], (DEPTH, D_MODEL, IN_WIDTH), D_MODEL ** -0.5) * in_col_scale
    q_norm_g = gain(ks[5], MLA_Q_RANK)
    w_uq = nrm(ks[6], (DEPTH, MLA_Q_RANK, MLA_HEADS * MLA_QK_DIM), MLA_Q_RANK ** -0.5)
    kv_norm_g = gain(ks[7], MLA_KV_RANK)
    ukv_scale = jnp.tile(jnp.concatenate([jnp.ones((MLA_NOPE_DIM,), f32),
                                          jnp.full((MLA_V_DIM,), beta, f32)]), MLA_HEADS)
    w_ukv = nrm(ks[8], (DEPTH, MLA_KV_RANK, MLA_HEADS * (MLA_NOPE_DIM + MLA_V_DIM)),
                MLA_KV_RANK ** -0.5) * ukv_scale
    out_norm_g = gain(ks[9], MIX_WIDTH)
    w_o = nrm(ks[10], (DEPTH, MIX_WIDTH, D_MODEL), beta * MIX_WIDTH ** -0.5)
    ln1_g = gain(ks[11], D_MODEL)
    ln1_b = 0.02 * jax.random.normal(ks[12], (DEPTH, D_MODEL), f32)
    w_up = nrm(ks[13], (DEPTH, D_MODEL, 2 * D_FF), beta * D_MODEL ** -0.5)
    conv_w = nrm(ks[14], (DEPTH, CONV_WIDTH, 2 * D_FF), CONV_WIDTH ** -0.5)
    conv_b = 0.01 * jax.random.normal(ks[15], (DEPTH, 2 * D_FF), f32)
    w_down = nrm(ks[16], (DEPTH, D_FF, D_MODEL), beta * D_FF ** -0.5)
    ln2_g = gain(ks[17], D_MODEL)
    ln2_b = 0.02 * jax.random.normal(ks[18], (DEPTH, D_MODEL), f32)
    return {'x': x, 'positions': positions, 'ln_emb_g': ln_emb_g, 'ln_emb_b': ln_emb_b,
            'w_in': w_in, 'q_norm_g': q_norm_g, 'w_uq': w_uq, 'kv_norm_g': kv_norm_g,
            'w_ukv': w_ukv, 'out_norm_g': out_norm_g, 'w_o': w_o, 'ln1_g': ln1_g, 'ln1_b': ln1_b,
            'w_up': w_up, 'conv_w': conv_w, 'conv_b': conv_b, 'w_down': w_down,
            'ln2_g': ln2_g, 'ln2_b': ln2_b}


def reference(x, positions, ln_emb_g, ln_emb_b, w_in, q_norm_g, w_uq, kv_norm_g, w_ukv,
              out_norm_g, w_o, ln1_g, ln1_b, w_up, conv_w, conv_b, w_down, ln2_g, ln2_b):
    x = layer_norm(x, ln_emb_g, ln_emb_b)
    for l in range(DEPTH):
        x = hybrid_layer(x, positions, w_in[l], q_norm_g[l], w_uq[l], kv_norm_g[l], w_ukv[l],
                         out_norm_g[l], w_o[l], ln1_g[l], ln1_b[l], w_up[l], conv_w[l],
                         conv_b[l], w_down[l], ln2_g[l], ln2_b[l])
    return x
```

```python
import functools
import math

import jax
import jax.numpy as jnp
from jax import lax
from jax.experimental import pallas as pl
from jax.experimental.pallas import tpu as pltpu

F32 = jnp.float32
BF16 = jnp.bfloat16

SWA_HEAD_DIM = 64
SWA_PATTERNS = ((128, 1), (512, 4), (2048, 16))
SWA_ROT_DIM = 16
MLA_NOPE_DIM = 128
MLA_ROPE_DIM = 64
MLA_V_DIM = 128
MLA_QK_DIM = MLA_NOPE_DIM + MLA_ROPE_DIM
MLA_HEADS = 4
MLA_QK_PAD = 256
ROPE_THETA = 500000.0
LN_EPS = 1e-5
RMS_EPS = 1e-6
NEG_INF = -1e30
CONV_WIDTH = 3

LANES = 128
HALO = 8
BF16_ROWS = 16
VMEM_LIMIT = 56 * 1024 * 1024


def _dot(a, b):
    return jnp.dot(a, b, preferred_element_type=F32)


def _dot_nt(a, b):
    return lax.dot_general(a, b, (((1,), (1,)), ((), ())), preferred_element_type=F32)


def _layer_norm(x, g, b):
    mu = jnp.mean(x, axis=-1, keepdims=True)
    xc = x - mu
    var = jnp.mean(xc * xc, axis=-1, keepdims=True)
    return xc * lax.rsqrt(var + LN_EPS) * g + b


def _rms_norm(x, g):
    return x * lax.rsqrt(jnp.mean(x * x, axis=-1, keepdims=True) + RMS_EPS) * g


def _proj_kernel(x_ref, pos_ref, invfa_ref, invfm_ref, lng_ref, lnb_ref, win_ref, qng_ref, kvng_ref,
                 wuqt_ref, wuk_ref, wuvt_ref,
                 qa_ref, ka_ref, va_ref, qt_ref, kf_ref, vt_ref, *, swa_w, q_rank, kv_rank):
    tm = x_ref.shape[1]
    xn = _layer_norm(x_ref[0], lng_ref[...], lnb_ref[...])
    h = _dot(xn.astype(BF16), win_ref[...])

    pos = pos_ref[0]
    ang_a = invfa_ref[...] * pos
    cos_a, sin_a = jnp.cos(ang_a), jnp.sin(ang_a)
    half = SWA_ROT_DIM // 2
    rest = SWA_HEAD_DIM - SWA_ROT_DIM
    ones = jnp.ones((rest, tm), F32)
    zeros = jnp.zeros((rest, tm), F32)
    zhalf = jnp.zeros((half, tm), F32)
    reps = LANES // SWA_HEAD_DIM
    cos_t = jnp.concatenate([cos_a, cos_a, ones] * reps, axis=0).T
    sin_lo = jnp.concatenate([-sin_a, zhalf, zeros] * reps, axis=0).T
    sin_hi = jnp.concatenate([zhalf, sin_a, zeros] * reps, axis=0).T

    def rope_a(xs):
        return (xs * cos_t + pltpu.roll(xs, LANES - half, 1) * sin_lo + pltpu.roll(xs, half, 1) * sin_hi)

    qscale = SWA_HEAD_DIM ** -0.5
    for j in range(swa_w // LANES):
        c0 = j * LANES
        qa_ref[0, :, c0:c0 + LANES] = (rope_a(h[:, c0:c0 + LANES]) * qscale).astype(BF16)
        ka_ref[0, :, c0:c0 + LANES] = rope_a(h[:, swa_w + c0:swa_w + c0 + LANES]).astype(BF16)
    va_ref[0] = h[:, 2 * swa_w:3 * swa_w].astype(BF16)

    o_cq = 3 * swa_w
    o_ckv = o_cq + q_rank
    o_kr = o_ckv + kv_rank
    cqn = _rms_norm(h[:, o_cq:o_ckv], qng_ref[...]).astype(BF16)
    ckvn = _rms_norm(h[:, o_ckv:o_kr], kvng_ref[...]).astype(BF16)
    ang_m = invfm_ref[...] * pos
    cos_m, sin_m = jnp.cos(ang_m), jnp.sin(ang_m)
    hm = MLA_ROPE_DIM // 2

    def rope_t(x1, x2):
        return x1 * cos_m - x2 * sin_m, x2 * cos_m + x1 * sin_m

    qt = _dot_nt(wuqt_ref[...], cqn) * (MLA_QK_DIM ** -0.5)
    for hd in range(MLA_HEADS):
        r0 = hd * MLA_QK_PAD
        qt_ref[0, hd, 0:MLA_NOPE_DIM, :] = qt[r0:r0 + MLA_NOPE_DIM].astype(BF16)
        p0 = r0 + MLA_NOPE_DIM
        n1, n2 = rope_t(qt[p0:p0 + hm], qt[p0 + hm:p0 + 2 * hm])
        qt_ref[0, hd, MLA_NOPE_DIM:MLA_NOPE_DIM + hm, :] = n1.astype(BF16)
        qt_ref[0, hd, MLA_NOPE_DIM + hm:MLA_QK_DIM, :] = n2.astype(BF16)
        qt_ref[0, hd, MLA_QK_DIM:MLA_QK_PAD, :] = jnp.zeros((MLA_QK_PAD - MLA_QK_DIM, tm), BF16)

    kr_t = h[:, o_kr:o_kr + LANES].T
    k1, k2 = rope_t(kr_t[0:hm], kr_t[hm:2 * hm])
    kpe = jnp.concatenate([k1, k2, kr_t[2 * hm:]], axis=0).T.astype(BF16)
    kn = _dot(ckvn, wuk_ref[...])
    for hd in range(MLA_HEADS):
        kf_ref[0, hd, :, 0:MLA_NOPE_DIM] = kn[:, hd * MLA_NOPE_DIM:(hd + 1) * MLA_NOPE_DIM].astype(BF16)
        kf_ref[0, hd, :, MLA_NOPE_DIM:MLA_QK_PAD] = kpe
    vt_ref[0, 0] = _dot_nt(wuvt_ref[...], ckvn).astype(BF16)


def _proj_call(x, pos_row, invf_a, invf_m, ln_g, ln_b, w_in_p, qn_g, kvn_g, wuqt, wuk, wuvt, *, tm):
    B, S, D = x.shape
    swa_w = 512
    q_rank = qn_g.shape[-1]
    kv_rank = kvn_g.shape[-1]
    nt = S // tm
    const = lambda *shape: pl.BlockSpec(shape, lambda b, i: (0,) * len(shape))
    out_shape = (
        jax.ShapeDtypeStruct((B, S, swa_w), BF16),
        jax.ShapeDtypeStruct((B, S, swa_w), BF16),
        jax.ShapeDtypeStruct((B, S, swa_w), BF16),
        jax.ShapeDtypeStruct((B, MLA_HEADS, MLA_QK_PAD, S), BF16),
        jax.ShapeDtypeStruct((B, MLA_HEADS, S, MLA_QK_PAD), BF16),
        jax.ShapeDtypeStruct((B, nt, MLA_HEADS * MLA_V_DIM, tm), BF16),
    )
    return pl.pallas_call(
        functools.partial(_proj_kernel, swa_w=swa_w, q_rank=q_rank, kv_rank=kv_rank),
        out_shape=out_shape,
        grid=(B, nt),
        in_specs=[
            pl.BlockSpec((1, tm, D), lambda b, i: (b, i, 0)),
            pl.BlockSpec((1, 1, tm), lambda b, i: (b, 0, i)),
            const(*invf_a.shape), const(*invf_m.shape), const(*ln_g.shape), const(*ln_b.shape),
            const(*w_in_p.shape), const(*qn_g.shape), const(*kvn_g.shape),
            const(*wuqt.shape), const(*wuk.shape), const(*wuvt.shape),
        ],
        out_specs=(
            pl.BlockSpec((1, tm, swa_w), lambda b, i: (b, i, 0)),
            pl.BlockSpec((1, tm, swa_w), lambda b, i: (b, i, 0)),
            pl.BlockSpec((1, tm, swa_w), lambda b, i: (b, i, 0)),
            pl.BlockSpec((1, MLA_HEADS, MLA_QK_PAD, tm), lambda b, i: (b, 0, 0, i)),
            pl.BlockSpec((1, MLA_HEADS, tm, MLA_QK_PAD), lambda b, i: (b, 0, i, 0)),
            pl.BlockSpec((1, 1, MLA_HEADS * MLA_V_DIM, tm), lambda b, i: (b, i, 0, 0)),
        ),
        compiler_params=pltpu.CompilerParams(
            dimension_semantics=("parallel", "parallel"), vmem_limit_bytes=VMEM_LIMIT),
        name="proj",
    )(x, pos_row, invf_a, invf_m, ln_g, ln_b, w_in_p, qn_g, kvn_g, wuqt, wuk, wuvt)


def _swa_kernel(q_ref, k_ref, v_ref, o_ref, stage, qd, kd, vd, op, lp, ot, lt, *, S, bq_max):
    dil = [d for _, d in SWA_PATTERNS]
    lane = lax.broadcasted_iota(jnp.int32, (1, LANES), 1)
    head0 = lane < SWA_HEAD_DIM

    for src, dst in ((q_ref, qd), (k_ref, kd), (v_ref, vd)):
        stage[...] = src[0].astype(F32)
        pi = 0
        for d in dil:
            if d == 1:
                continue
            L = S // d
            for r in range(d):
                dst[pi, r * L:(r + 1) * L, :] = stage[pl.ds(r, L, stride=d), :].astype(BF16)
            pi += 1

    pi = 0
    for p, (window, d) in enumerate(SWA_PATTERNS):
        L = S // d
        n_side = window // (2 * d)
        bq = min(bq_max, L)
        win = min(L, bq + 2 * n_side)
        nb = L // bq
        if d == 1:
            load_q = lambda r0, n: q_ref[0, pl.ds(r0, n), :]
            load_k = lambda r0, n: k_ref[0, pl.ds(r0, n), :]
            load_v = lambda r0, n: v_ref[0, pl.ds(r0, n), :]
        else:
            load_q = functools.partial(lambda r0, n, j: qd[j, pl.ds(r0, n), :], j=pi)
            load_k = functools.partial(lambda r0, n, j: kd[j, pl.ds(r0, n), :], j=pi)
            load_v = functools.partial(lambda r0, n, j: vd[j, pl.ds(r0, n), :], j=pi)
            pi += 1

        def body(n, carry, L=L, n_side=n_side, bq=bq, win=win, nb=nb, p=p,
                 load_q=load_q, load_k=load_k, load_v=load_v):
            r = n // nb
            i = n % nb
            row0 = pl.multiple_of(r * L + i * bq, bq)
            ws = jnp.clip(i * bq - n_side, 0, L - win)
            krow0 = pl.multiple_of(r * L + ws, BF16_ROWS)
            q = load_q(row0, bq)
            k = load_k(krow0, win)
            v = load_v(krow0, win)
            qpos = i * bq + lax.broadcasted_iota(jnp.int32, (bq, win), 0)
            kpos = ws + lax.broadcasted_iota(jnp.int32, (bq, win), 1)
            valid = jnp.abs(qpos - kpos) <= n_side
            outs, lses = [], []
            for hsel in (head0, jnp.logical_not(head0)):
                qh = jnp.where(hsel, q, jnp.zeros_like(q))
                s = _dot_nt(qh, k)
                s = jnp.where(valid, s, NEG_INF)
                m = jnp.max(s, axis=1, keepdims=True)
                e = jnp.exp(s - m)
                den = jnp.sum(e, axis=1, keepdims=True)
                outs.append(_dot(e.astype(BF16), v) / den)
                lses.append(m + jnp.log(den))
            op[p, pl.ds(row0, bq), :] = jnp.where(head0, outs[0], outs[1])
            lp[p, pl.ds(row0, bq), :] = jnp.where(head0, lses[0], lses[1])
            return carry

        lax.fori_loop(0, S // bq, body, 0)

    pi = 0
    for p, (_, d) in enumerate(SWA_PATTERNS):
        if d == 1:
            continue
        L = S // d
        for r in range(d):
            ot[pi, pl.ds(r, L, stride=d), :] = op[p, r * L:(r + 1) * L, :]
            lt[pi, pl.ds(r, L, stride=d), :] = lp[p, r * L:(r + 1) * L, :]
        pi += 1

    cr = min(512, S)

    def combine(c, carry):
        r0 = pl.multiple_of(c * cr, cr)
        os_, ls_ = [], []
        pj = 0
        for p, (_, d) in enumerate(SWA_PATTERNS):
            if d == 1:
                os_.append(op[p, pl.ds(r0, cr), :])
                ls_.append(lp[p, pl.ds(r0, cr), :])
            else:
                os_.append(ot[pj, pl.ds(r0, cr), :])
                ls_.append(lt[pj, pl.ds(r0, cr), :])
                pj += 1
        m = functools.reduce(jnp.maximum, ls_)
        es = [jnp.exp(l - m) for l in ls_]
        den = functools.reduce(lambda a, b: a + b, es)
        num = functools.reduce(lambda a, b: a + b, [e * o for e, o in zip(es, os_)])
        o_ref[0, pl.ds(r0, cr), :] = (num / den).astype(o_ref.dtype)
        return carry

    lax.fori_loop(0, S // cr, combine, 0)


def _swa_call(q_a, k_a, v_a):
    B, S, W = q_a.shape
    nd = sum(1 for _, d in SWA_PATTERNS if d != 1)
    spec = pl.BlockSpec((1, S, LANES), lambda b, j: (b, 0, j))
    return pl.pallas_call(
        functools.partial(_swa_kernel, S=S, bq_max=128),
        out_shape=jax.ShapeDtypeStruct((B, S, W), BF16),
        grid=(B, W // LANES),
        in_specs=[spec, spec, spec],
        out_specs=spec,
        scratch_shapes=[
            pltpu.VMEM((S, LANES), F32),
            pltpu.VMEM((nd, S, LANES), BF16),
            pltpu.VMEM((nd, S, LANES), BF16),
            pltpu.VMEM((nd, S, LANES), BF16),
            pltpu.VMEM((len(SWA_PATTERNS), S, LANES), F32),
            pltpu.VMEM((len(SWA_PATTERNS), S, LANES), F32),
            pltpu.VMEM((nd, S, LANES), F32),
            pltpu.VMEM((nd, S, LANES), F32),
        ],
        compiler_params=pltpu.CompilerParams(
            dimension_semantics=("parallel", "parallel"), vmem_limit_bytes=VMEM_LIMIT),
        name="swa",
    )(q_a, k_a, v_a)


def _mla_kernel(qt_ref, k_ref, vt_ref, o_ref, acc_ref):
    qt = qt_ref[0, 0]
    tq = qt.shape[1]
    nk, _, tk = vt_ref.shape[1:]
    acc_ref[...] = jnp.zeros_like(acc_ref)

    def body(j, carry):
        m, l = carry
        k = k_ref[0, 0, pl.ds(pl.multiple_of(j * tk, tk), tk), :]
        st = _dot(k, qt)
        m_new = jnp.maximum(m, jnp.max(st, axis=0, keepdims=True))
        a = jnp.exp(m - m_new)
        e = jnp.exp(st - m_new)
        l = a * l + jnp.sum(e, axis=0, keepdims=True)
        acc_ref[...] = a * acc_ref[...] + _dot(vt_ref[0, j], e.astype(BF16))
        return m_new, l

    m0 = jnp.full((1, tq), NEG_INF, F32)
    l0 = jnp.zeros((1, tq), F32)
    _, l = lax.fori_loop(0, nk, body, (m0, l0))
    o_ref[0] = (acc_ref[...] / l).T.astype(o_ref.dtype)


def _mla_call(qt, kf, vt, *, tq):
    B, H, QP, S = qt.shape
    nk, _, tk = vt.shape[1:]
    return pl.pallas_call(
        _mla_kernel,
        out_shape=jax.ShapeDtypeStruct((B, S, H * MLA_V_DIM), BF16),
        grid=(B, H, S // tq),
        in_specs=[
            pl.BlockSpec((1, 1, QP, tq), lambda b, h, i: (b, h, 0, i)),
            pl.BlockSpec((1, 1, S, QP), lambda b, h, i: (b, h, 0, 0)),
            pl.BlockSpec((1, nk, MLA_V_DIM, tk), lambda b, h, i: (b, 0, h, 0)),
        ],
        out_specs=pl.BlockSpec((1, tq, MLA_V_DIM), lambda b, h, i: (b, i, h)),
        scratch_shapes=[pltpu.VMEM((MLA_V_DIM, tq), F32)],
        compiler_params=pltpu.CompilerParams(
            dimension_semantics=("parallel", "parallel", "parallel"), vmem_limit_bytes=VMEM_LIMIT),
        name="mla",
    )(qt, kf, vt)


def _ffn_kernel(x_ref, xp_ref, xn_ref, oa_ref, oap_ref, oan_ref, ob_ref, obp_ref, obn_ref,
                lneg_ref, lneb_ref, ong_ref, wo_ref, ln1g_ref, ln1b_ref, wup_ref, cw_ref, cb_ref,
                wdn_ref, ln2g_ref, ln2b_ref, out_ref, u_ref, *, alpha, d_ff, n_chunks):
    tm = x_ref.shape[1]
    rows = tm + 2 * HALO
    i = pl.program_id(1)
    last = pl.num_programs(1) - 1
    hi = BF16_ROWS - HALO

    def with_halo(main, prev, nxt):
        return jnp.concatenate([prev, main, nxt], axis=0)

    x = with_halo(x_ref[0], xp_ref[0], xn_ref[0])
    oa = with_halo(oa_ref[0].astype(F32), oap_ref[0].astype(F32)[hi:], oan_ref[0].astype(F32)[:HALO])
    ob = with_halo(ob_ref[0].astype(F32), obp_ref[0].astype(F32)[hi:], obn_ref[0].astype(F32)[:HALO])
    wa = oa.shape[1]
    xn = _layer_norm(x, lneg_ref[...], lneb_ref[...])
    o = jnp.concatenate([_rms_norm(oa, ong_ref[:, :wa]), _rms_norm(ob, ong_ref[:, wa:])], axis=1)
    x1 = _layer_norm(alpha * xn + _dot(o.astype(BF16), wo_ref[...]), ln1g_ref[...], ln1b_ref[...])

    ridx = lax.broadcasted_iota(jnp.int32, (rows, 1), 0)
    inside = jnp.logical_and(jnp.logical_or(ridx >= HALO, i > 0),
                             jnp.logical_or(ridx < tm + HALO, i < last))
    x1b = jnp.where(inside, x1, 0.0).astype(BF16)

    ck = d_ff // n_chunks
    y = None
    for c in range(n_chunks):
        parts = []
        for base in (c * ck, d_ff + c * ck):
            u_ref[...] = _dot(x1b, wup_ref[:, base:base + ck])
            cw = cw_ref[:, base:base + ck]
            parts.append(cb_ref[:, base:base + ck]
                         + u_ref[HALO - 1:HALO - 1 + tm, :] * cw[0:1]
                         + u_ref[HALO:HALO + tm, :] * cw[1:2]
                         + u_ref[HALO + 1:HALO + 1 + tm, :] * cw[2:3])
        gate, val = parts
        g = (gate / (1.0 + jnp.exp(-gate)) * val).astype(BF16)
        yc = _dot(g, wdn_ref[c * ck:(c + 1) * ck, :])
        y = yc if y is None else y + yc
    out_ref[0] = _layer_norm(alpha * x1[HALO:HALO + tm] + y, ln2g_ref[...], ln2b_ref[...])


def _ffn_call(x, o_a, o_b, lne_g, lne_b, on_g, w_o, ln1_g, ln1_b, w_up, conv_w, conv_b, w_down,
              ln2_g, ln2_b, *, tm, alpha, n_chunks):
    B, S, D = x.shape
    d_ff = w_down.shape[0]
    wa, wb = o_a.shape[-1], o_b.shape[-1]
    nt = S // tm
    r8, r16 = tm // HALO, tm // BF16_ROWS
    const = lambda *shape: pl.BlockSpec(shape, lambda b, i: (0,) * len(shape), pipeline_mode=pl.Buffered(1))

    def main(w):
        return pl.BlockSpec((1, tm, w), lambda b, i: (b, i, 0))

    def prev(w, rb, r):
        return pl.BlockSpec((1, rb, w), lambda b, i: (b, jnp.maximum(i * r - 1, 0), 0))

    def nxt(w, rb, r):
        return pl.BlockSpec((1, rb, w), lambda b, i: (b, jnp.minimum((i + 1) * r, S // rb - 1), 0))

    return pl.pallas_call(
        functools.partial(_ffn_kernel, alpha=alpha, d_ff=d_ff, n_chunks=n_chunks),
        out_shape=jax.ShapeDtypeStruct((B, S, D), F32),
        grid=(B, nt),
        in_specs=[
            main(D), prev(D, HALO, r8), nxt(D, HALO, r8),
            main(wa), prev(wa, BF16_ROWS, r16), nxt(wa, BF16_ROWS, r16),
            main(wb), prev(wb, BF16_ROWS, r16), nxt(wb, BF16_ROWS, r16),
            const(*lne_g.shape), const(*lne_b.shape), const(*on_g.shape), const(*w_o.shape),
            const(*ln1_g.shape), const(*ln1_b.shape), const(*w_up.shape), const(*conv_w.shape),
            const(*conv_b.shape), const(*w_down.shape), const(*ln2_g.shape), const(*ln2_b.shape),
        ],
        out_specs=pl.BlockSpec((1, tm, D), lambda b, i: (b, i, 0)),
        scratch_shapes=[pltpu.VMEM((tm + 2 * HALO, d_ff // n_chunks), F32)],
        compiler_params=pltpu.CompilerParams(
            dimension_semantics=("parallel", "parallel"), vmem_limit_bytes=VMEM_LIMIT),
        name="ffn",
    )(x, x, x, o_a, o_a, o_a, o_b, o_b, o_b, lne_g, lne_b, on_g, w_o, ln1_g, ln1_b, w_up, conv_w, conv_b,
      w_down, ln2_g, ln2_b)


def _layer(x, positions, ln_emb_g, ln_emb_b, w_in, q_norm_g, w_uq, kv_norm_g, w_ukv, out_norm_g, w_o,
           ln1_g, ln1_b, w_up, conv_w, conv_b, w_down, ln2_g, ln2_b, *, alpha, tm_proj, tq, tm_ffn, n_chunks):
    B, S, D = x.shape
    row = lambda v: v.reshape(1, -1).astype(F32)
    w_in_p = jnp.concatenate([w_in, jnp.zeros((D, LANES - MLA_ROPE_DIM), w_in.dtype)], axis=1).astype(BF16)
    q_rank = w_uq.shape[0]
    wq = w_uq.reshape(q_rank, MLA_HEADS, MLA_QK_DIM)
    wq = jnp.pad(wq, ((0, 0), (0, 0), (0, MLA_QK_PAD - MLA_QK_DIM)))
    wuqt = wq.reshape(q_rank, MLA_HEADS * MLA_QK_PAD).T.astype(BF16)
    kv_rank = w_ukv.shape[0]
    wkv = w_ukv.reshape(kv_rank, MLA_HEADS, MLA_NOPE_DIM + MLA_V_DIM)
    wuk = wkv[:, :, :MLA_NOPE_DIM].reshape(kv_rank, MLA_HEADS * MLA_NOPE_DIM).astype(BF16)
    wuvt = wkv[:, :, MLA_NOPE_DIM:].reshape(kv_rank, MLA_HEADS * MLA_V_DIM).T.astype(BF16)

    def inv_freq(rot_dim):
        half = rot_dim // 2
        return jnp.power(jnp.float32(ROPE_THETA),
                         -jnp.arange(half, dtype=F32) * (2.0 / rot_dim)).reshape(half, 1)

    pos_row = positions.astype(F32).reshape(B, 1, S)
    q_a, k_a, v_a, qt, kf, vt = _proj_call(
        x, pos_row, inv_freq(SWA_ROT_DIM), inv_freq(MLA_ROPE_DIM), row(ln_emb_g), row(ln_emb_b), w_in_p,
        row(q_norm_g), row(kv_norm_g), wuqt, wuk, wuvt, tm=tm_proj)
    o_a = _swa_call(q_a, k_a, v_a)
    o_b = _mla_call(qt, kf, vt, tq=tq)
    return _ffn_call(x, o_a, o_b, row(ln_emb_g), row(ln_emb_b), row(out_norm_g), w_o.astype(BF16),
                     row(ln1_g), row(ln1_b), w_up.astype(BF16), conv_w.astype(F32), row(conv_b),
                     w_down.astype(BF16), row(ln2_g), row(ln2_b), tm=tm_ffn, alpha=alpha, n_chunks=n_chunks)


def kernel(x, positions, ln_emb_g, ln_emb_b, w_in, q_norm_g, w_uq, kv_norm_g, w_ukv, out_norm_g, w_o,
           ln1_g, ln1_b, w_up, conv_w, conv_b, w_down, ln2_g, ln2_b):
    depth = w_in.shape[0]
    assert depth == 1, "single-layer stack"
    alpha = (2.0 * depth) ** 0.25
    S = x.shape[1]
    tile = min(512, S)
    return _layer(x, positions, ln_emb_g, ln_emb_b, w_in[0], q_norm_g[0], w_uq[0], kv_norm_g[0], w_ukv[0],
                  out_norm_g[0], w_o[0], ln1_g[0], ln1_b[0], w_up[0], conv_w[0], conv_b[0], w_down[0],
                  ln2_g[0], ln2_b[0], alpha=alpha, tm_proj=tile, tq=tile, tm_ffn=tile, n_chunks=2)
```

```python
import functools
import math

import jax
import jax.numpy as jnp
from jax import lax
from jax.experimental import pallas as pl
from jax.experimental.pallas import tpu as pltpu

F32 = jnp.float32
BF16 = jnp.bfloat16

SWA_HEAD_DIM = 64
SWA_PATTERNS = ((128, 1), (512, 4), (2048, 16))
SWA_ROT_DIM = 16
MLA_NOPE_DIM = 128
MLA_ROPE_DIM = 64
MLA_V_DIM = 128
MLA_QK_DIM = MLA_NOPE_DIM + MLA_ROPE_DIM
MLA_HEADS = 4
MLA_QK_PAD = 256
ROPE_THETA = 500000.0
LN_EPS = 1e-5
RMS_EPS = 1e-6
NEG_INF = -1e30
CONV_WIDTH = 3
LOG2E = math.log2(math.e)

LANES = 128
HALO = 8
BF16_ROWS = 16
MLA_VT_ROWS = MLA_V_DIM + BF16_ROWS
VMEM_LIMIT = 56 * 1024 * 1024


def _dot(a, b):
    return jnp.dot(a, b, preferred_element_type=F32)


def _dot_nt(a, b):
    return lax.dot_general(a, b, (((1,), (1,)), ((), ())), preferred_element_type=F32)


def _layer_norm(x, g, b):
    mu = jnp.mean(x, axis=-1, keepdims=True)
    xc = x - mu
    var = jnp.mean(xc * xc, axis=-1, keepdims=True)
    return xc * lax.rsqrt(var + LN_EPS) * g + b


def _rms_norm(x, g):
    return x * lax.rsqrt(jnp.mean(x * x, axis=-1, keepdims=True) + RMS_EPS) * g


def _proj_kernel(x_ref, pos_ref, invfa_ref, invfm_ref, lng_ref, lnb_ref, win_ref, qng_ref, kvng_ref,
                 wuqt_ref, wuk_ref, wuvt_ref,
                 qa_ref, ka_ref, va_ref, qt_ref, kf_ref, vt_ref, *, swa_w, q_rank, kv_rank):
    tm = x_ref.shape[1]
    xn = _layer_norm(x_ref[0], lng_ref[...], lnb_ref[...])
    h = _dot(xn.astype(BF16), win_ref[...])

    pos = pos_ref[0]
    ang_a = invfa_ref[...] * pos
    cos_a, sin_a = jnp.cos(ang_a), jnp.sin(ang_a)
    half = SWA_ROT_DIM // 2
    rest = SWA_HEAD_DIM - SWA_ROT_DIM
    ones = jnp.ones((rest, tm), F32)
    zeros = jnp.zeros((rest, tm), F32)
    zhalf = jnp.zeros((half, tm), F32)
    reps = LANES // SWA_HEAD_DIM
    cos_t = jnp.concatenate([cos_a, cos_a, ones] * reps, axis=0).T
    sin_lo = jnp.concatenate([-sin_a, zhalf, zeros] * reps, axis=0).T
    sin_hi = jnp.concatenate([zhalf, sin_a, zeros] * reps, axis=0).T

    def rope_a(xs):
        return (xs * cos_t + pltpu.roll(xs, LANES - half, 1) * sin_lo + pltpu.roll(xs, half, 1) * sin_hi)

    qscale = SWA_HEAD_DIM ** -0.5 * LOG2E
    for j in range(swa_w // LANES):
        c0 = j * LANES
        qa_ref[0, :, c0:c0 + LANES] = (rope_a(h[:, c0:c0 + LANES]) * qscale).astype(BF16)
        ka_ref[0, :, c0:c0 + LANES] = rope_a(h[:, swa_w + c0:swa_w + c0 + LANES]).astype(BF16)
    va_ref[0] = h[:, 2 * swa_w:3 * swa_w].astype(BF16)

    o_cq = 3 * swa_w
    o_ckv = o_cq + q_rank
    o_kr = o_ckv + kv_rank
    cqn = _rms_norm(h[:, o_cq:o_ckv], qng_ref[...]).astype(BF16)
    ckvn = _rms_norm(h[:, o_ckv:o_kr], kvng_ref[...]).astype(BF16)
    ang_m = invfm_ref[...] * pos
    cos_m, sin_m = jnp.cos(ang_m), jnp.sin(ang_m)
    hm = MLA_ROPE_DIM // 2

    def rope_t(x1, x2):
        return x1 * cos_m - x2 * sin_m, x2 * cos_m + x1 * sin_m

    qt = _dot_nt(wuqt_ref[...], cqn) * (MLA_QK_DIM ** -0.5 * LOG2E)
    for hd in range(MLA_HEADS):
        r0 = hd * MLA_QK_PAD
        qt_ref[0, hd, 0:MLA_NOPE_DIM, :] = qt[r0:r0 + MLA_NOPE_DIM].astype(BF16)
        p0 = r0 + MLA_NOPE_DIM
        n1, n2 = rope_t(qt[p0:p0 + hm], qt[p0 + hm:p0 + 2 * hm])
        qt_ref[0, hd, MLA_NOPE_DIM:MLA_NOPE_DIM + hm, :] = n1.astype(BF16)
        qt_ref[0, hd, MLA_NOPE_DIM + hm:MLA_QK_DIM, :] = n2.astype(BF16)
        qt_ref[0, hd, MLA_QK_DIM:MLA_QK_PAD, :] = jnp.zeros((MLA_QK_PAD - MLA_QK_DIM, tm), BF16)

    kr_t = h[:, o_kr:o_kr + LANES].T
    k1, k2 = rope_t(kr_t[0:hm], kr_t[hm:2 * hm])
    kpe = jnp.concatenate([k1, k2, kr_t[2 * hm:]], axis=0).T.astype(BF16)
    kn = _dot(ckvn, wuk_ref[...])
    for hd in range(MLA_HEADS):
        kf_ref[0, hd, :, 0:MLA_NOPE_DIM] = kn[:, hd * MLA_NOPE_DIM:(hd + 1) * MLA_NOPE_DIM].astype(BF16)
        kf_ref[0, hd, :, MLA_NOPE_DIM:MLA_QK_PAD] = kpe
    vt = _dot_nt(wuvt_ref[...], ckvn)
    for hd in range(MLA_HEADS):
        vt_ref[0, 0, hd, 0:MLA_V_DIM, :] = vt[hd * MLA_V_DIM:(hd + 1) * MLA_V_DIM].astype(BF16)
        vt_ref[0, 0, hd, MLA_V_DIM:MLA_VT_ROWS, :] = jnp.ones((MLA_VT_ROWS - MLA_V_DIM, tm), BF16)


def _proj_call(x, pos_row, invf_a, invf_m, ln_g, ln_b, w_in_p, qn_g, kvn_g, wuqt, wuk, wuvt, *, tm):
    B, S, D = x.shape
    swa_w = 512
    q_rank = qn_g.shape[-1]
    kv_rank = kvn_g.shape[-1]
    nt = S // tm
    const = lambda *shape: pl.BlockSpec(shape, lambda b, i: (0,) * len(shape))
    out_shape = (
        jax.ShapeDtypeStruct((B, S, swa_w), BF16),
        jax.ShapeDtypeStruct((B, S, swa_w), BF16),
        jax.ShapeDtypeStruct((B, S, swa_w), BF16),
        jax.ShapeDtypeStruct((B, MLA_HEADS, MLA_QK_PAD, S), BF16),
        jax.ShapeDtypeStruct((B, MLA_HEADS, S, MLA_QK_PAD), BF16),
        jax.ShapeDtypeStruct((B, nt, MLA_HEADS, MLA_VT_ROWS, tm), BF16),
    )
    return pl.pallas_call(
        functools.partial(_proj_kernel, swa_w=swa_w, q_rank=q_rank, kv_rank=kv_rank),
        out_shape=out_shape,
        grid=(B, nt),
        in_specs=[
            pl.BlockSpec((1, tm, D), lambda b, i: (b, i, 0)),
            pl.BlockSpec((1, 1, tm), lambda b, i: (b, 0, i)),
            const(*invf_a.shape), const(*invf_m.shape), const(*ln_g.shape), const(*ln_b.shape),
            const(*w_in_p.shape), const(*qn_g.shape), const(*kvn_g.shape),
            const(*wuqt.shape), const(*wuk.shape), const(*wuvt.shape),
        ],
        out_specs=(
            pl.BlockSpec((1, tm, swa_w), lambda b, i: (b, i, 0)),
            pl.BlockSpec((1, tm, swa_w), lambda b, i: (b, i, 0)),
            pl.BlockSpec((1, tm, swa_w), lambda b, i: (b, i, 0)),
            pl.BlockSpec((1, MLA_HEADS, MLA_QK_PAD, tm), lambda b, i: (b, 0, 0, i)),
            pl.BlockSpec((1, MLA_HEADS, tm, MLA_QK_PAD), lambda b, i: (b, 0, i, 0)),
            pl.BlockSpec((1, 1, MLA_HEADS, MLA_VT_ROWS, tm), lambda b, i: (b, i, 0, 0, 0)),
        ),
        compiler_params=pltpu.CompilerParams(
            dimension_semantics=("parallel", "parallel"), vmem_limit_bytes=VMEM_LIMIT),
        name="proj",
    )(x, pos_row, invf_a, invf_m, ln_g, ln_b, w_in_p, qn_g, kvn_g, wuqt, wuk, wuvt)


def _swa_geometry(S, window, d, bq_max):
    L = S // d
    n_side = window // (2 * d)
    bq = min(bq_max, L)
    win = min(L, bq + 2 * n_side)
    return L, n_side, bq, win, L // bq


def _swa_kernel(q_ref, k_ref, v_ref, o_ref, stage, qd, kd, vd, bias, op, lp, ot, lt, *, S, bq_max, unroll):
    lane = lax.broadcasted_iota(jnp.int32, (1, LANES), 1)
    head0 = lane < SWA_HEAD_DIM

    for src, dst, lead in ((q_ref, qd, 0), (k_ref, kd, 0), (v_ref, vd, 1)):
        stage[...] = src[0].astype(F32)
        pi = lead
        for _, d in SWA_PATTERNS:
            if d == 1:
                continue
            L = S // d
            for r in range(d):
                dst[pi, r * L:(r + 1) * L, 0:LANES] = stage[pl.ds(r, L, stride=d), :].astype(BF16)
            pi += 1
    vd[0, :, 0:LANES] = v_ref[0]
    for p in range(len(SWA_PATTERNS)):
        vd[p, :, LANES:2 * LANES] = jnp.ones((S, LANES), BF16)

    for p, (window, d) in enumerate(SWA_PATTERNS):
        L, n_side, bq, win, nb = _swa_geometry(S, window, d, bq_max)
        rel = (lax.broadcasted_iota(jnp.int32, (bq, win), 0) - lax.broadcasted_iota(jnp.int32, (bq, win), 1))
        ws_last = min(max((nb - 1) * bq - n_side, 0), L - win)
        for c, delta in enumerate((0, n_side, (nb - 1) * bq - ws_last)):
            bias[p, c, 0:bq, 0:win] = jnp.where(jnp.abs(rel + delta) <= n_side, 0.0, NEG_INF)

    pi = 0
    for p, (window, d) in enumerate(SWA_PATTERNS):
        L, n_side, bq, win, nb = _swa_geometry(S, window, d, bq_max)
        if d == 1:
            load_q = lambda r0, n: q_ref[0, pl.ds(r0, n), :]
            load_k = lambda r0, n: k_ref[0, pl.ds(r0, n), :]
        else:
            load_q = functools.partial(lambda r0, n, j: qd[j, pl.ds(r0, n), :], j=pi)
            load_k = functools.partial(lambda r0, n, j: kd[j, pl.ds(r0, n), :], j=pi)
            pi += 1

        def body(n, carry, L=L, n_side=n_side, bq=bq, win=win, nb=nb, p=p, load_q=load_q, load_k=load_k):
            r = n // nb
            i = n % nb
            row0 = pl.multiple_of(r * L + i * bq, bq)
            ws = jnp.clip(i * bq - n_side, 0, L - win)
            krow0 = pl.multiple_of(r * L + ws, BF16_ROWS)
            q = load_q(row0, bq)
            k = load_k(krow0, win)
            v1 = vd[p, pl.ds(krow0, win), :]
            mask = bias[p, jnp.where(i == 0, 0, jnp.where(i == nb - 1, 2, 1)), 0:bq, 0:win]
            outs, lses = [], []
            for hsel in (head0, jnp.logical_not(head0)):
                qh = jnp.where(hsel, q, jnp.zeros_like(q))
                s = _dot_nt(qh, k) + mask
                m = jnp.max(s, axis=1, keepdims=True)
                e = jnp.exp2(s - m).astype(BF16)
                oa = _dot(e, v1)
                den = oa[:, LANES:]
                outs.append(oa[:, :LANES] / den)
                lses.append(m + jnp.log2(den))
            op[p, pl.ds(row0, bq), :] = jnp.where(head0, outs[0], outs[1])
            lp[p, pl.ds(row0, bq), :] = jnp.where(head0, lses[0], lses[1])
            return carry

        lax.fori_loop(0, S // bq, body, 0, unroll=unroll)

    pi = 0
    for p, (_, d) in enumerate(SWA_PATTERNS):
        if d == 1:
            continue
        L = S // d
        for r in range(d):
            ot[pi, pl.ds(r, L, stride=d), :] = op[p, r * L:(r + 1) * L, :]
            lt[pi, pl.ds(r, L, stride=d), :] = lp[p, r * L:(r + 1) * L, :]
        pi += 1

    cr = min(512, S)

    def combine(c, carry):
        r0 = pl.multiple_of(c * cr, cr)
        os_, ls_ = [], []
        pj = 0
        for p, (_, d) in enumerate(SWA_PATTERNS):
            if d == 1:
                os_.append(op[p, pl.ds(r0, cr), :])
                ls_.append(lp[p, pl.ds(r0, cr), :])
            else:
                os_.append(ot[pj, pl.ds(r0, cr), :])
                ls_.append(lt[pj, pl.ds(r0, cr), :])
                pj += 1
        m = functools.reduce(jnp.maximum, ls_)
        es = [jnp.exp2(l - m) for l in ls_]
        den = functools.reduce(lambda a, b: a + b, es)
        num = functools.reduce(lambda a, b: a + b, [e * o for e, o in zip(es, os_)])
        o_ref[0, pl.ds(r0, cr), :] = (num / den).astype(o_ref.dtype)
        return carry

    lax.fori_loop(0, S // cr, combine, 0)


def _swa_call(q_a, k_a, v_a):
    B, S, W = q_a.shape
    npat = len(SWA_PATTERNS)
    nd = sum(1 for _, d in SWA_PATTERNS if d != 1)
    bq_max = 128
    geo = [_swa_geometry(S, w, d, bq_max) for w, d in SWA_PATTERNS]
    bq, win = max(g[2] for g in geo), max(g[3] for g in geo)
    spec = pl.BlockSpec((1, S, LANES), lambda b, j: (b, 0, j))
    return pl.pallas_call(
        functools.partial(_swa_kernel, S=S, bq_max=bq_max, unroll=8),
        out_shape=jax.ShapeDtypeStruct((B, S, W), BF16),
        grid=(B, W // LANES),
        in_specs=[spec, spec, spec],
        out_specs=spec,
        scratch_shapes=[
            pltpu.VMEM((S, LANES), F32),
            pltpu.VMEM((nd, S, LANES), BF16),
            pltpu.VMEM((nd, S, LANES), BF16),
            pltpu.VMEM((npat, S, 2 * LANES), BF16),
            pltpu.VMEM((npat, 3, bq, win), F32),
            pltpu.VMEM((npat, S, LANES), F32),
            pltpu.VMEM((npat, S, LANES), F32),
            pltpu.VMEM((nd, S, LANES), F32),
            pltpu.VMEM((nd, S, LANES), F32),
        ],
        compiler_params=pltpu.CompilerParams(
            dimension_semantics=("parallel", "parallel"), vmem_limit_bytes=VMEM_LIMIT),
        name="swa",
    )(q_a, k_a, v_a)


def _mla_kernel(qt_ref, k_ref, vt_ref, o_ref, acc_ref):
    qt = qt_ref[0, 0]
    tq = qt.shape[1]
    nk, tk = vt_ref.shape[1], vt_ref.shape[4]
    acc_ref[...] = jnp.zeros_like(acc_ref)

    def scores(j):
        return _dot(k_ref[0, 0, j * tk:(j + 1) * tk, :], qt)

    st = scores(0)
    m = jnp.full((1, tq), NEG_INF, F32)
    for j in range(nk):
        st_next = scores(j + 1) if j + 1 < nk else None
        m_new = jnp.maximum(m, jnp.max(st, axis=0, keepdims=True))
        a = jnp.exp2(m - m_new)
        e = jnp.exp2(st - m_new).astype(BF16)
        acc_ref[...] = a * acc_ref[...] + _dot(vt_ref[0, j, 0], e)
        st, m = st_next, m_new
    acc = acc_ref[...]
    o_ref[0] = (acc[:MLA_V_DIM] / acc[MLA_V_DIM:MLA_V_DIM + 1]).T.astype(o_ref.dtype)


def _mla_call(qt, kf, vt, *, tq):
    B, H, QP, S = qt.shape
    nk, tk = vt.shape[1], vt.shape[4]
    return pl.pallas_call(
        _mla_kernel,
        out_shape=jax.ShapeDtypeStruct((B, S, H * MLA_V_DIM), BF16),
        grid=(B, H, S // tq),
        in_specs=[
            pl.BlockSpec((1, 1, QP, tq), lambda b, h, i: (b, h, 0, i)),
            pl.BlockSpec((1, 1, S, QP), lambda b, h, i: (b, h, 0, 0)),
            pl.BlockSpec((1, nk, 1, MLA_VT_ROWS, tk), lambda b, h, i: (b, 0, h, 0, 0)),
        ],
        out_specs=pl.BlockSpec((1, tq, MLA_V_DIM), lambda b, h, i: (b, i, h)),
        scratch_shapes=[pltpu.VMEM((MLA_VT_ROWS, tq), F32)],
        compiler_params=pltpu.CompilerParams(
            dimension_semantics=("parallel", "parallel", "parallel"), vmem_limit_bytes=VMEM_LIMIT),
        name="mla",
    )(qt, kf, vt)


def _ffn_kernel(x_ref, xp_ref, xn_ref, oa_ref, oap_ref, oan_ref, ob_ref, obp_ref, obn_ref,
                lneg_ref, lneb_ref, ong_ref, wo_ref, ln1g_ref, ln1b_ref, wup_ref, cw_ref, cb_ref,
                wdn_ref, ln2g_ref, ln2b_ref, out_ref, u_ref, *, alpha, d_ff, n_chunks):
    tm = x_ref.shape[1]
    rows = tm + 2 * HALO
    i = pl.program_id(1)
    last = pl.num_programs(1) - 1
    hi = BF16_ROWS - HALO

    def with_halo(main, prev, nxt):
        return jnp.concatenate([prev, main, nxt], axis=0)

    x = with_halo(x_ref[0], xp_ref[0], xn_ref[0])
    oa = with_halo(oa_ref[0].astype(F32), oap_ref[0].astype(F32)[hi:], oan_ref[0].astype(F32)[:HALO])
    ob = with_halo(ob_ref[0].astype(F32), obp_ref[0].astype(F32)[hi:], obn_ref[0].astype(F32)[:HALO])
    wa = oa.shape[1]
    xn = _layer_norm(x, lneg_ref[...], lneb_ref[...])
    o = jnp.concatenate([_rms_norm(oa, ong_ref[:, :wa]), _rms_norm(ob, ong_ref[:, wa:])], axis=1)
    x1 = _layer_norm(alpha * xn + _dot(o.astype(BF16), wo_ref[...]), ln1g_ref[...], ln1b_ref[...])

    ridx = lax.broadcasted_iota(jnp.int32, (rows, 1), 0)
    inside = jnp.logical_and(jnp.logical_or(ridx >= HALO, i > 0),
                             jnp.logical_or(ridx < tm + HALO, i < last))
    x1b = jnp.where(inside, x1, 0.0).astype(BF16)

    ck = d_ff // n_chunks
    y = None
    for c in range(n_chunks):
        parts = []
        for base in (c * ck, d_ff + c * ck):
            u_ref[...] = _dot(x1b, wup_ref[:, base:base + ck])
            cw = cw_ref[:, base:base + ck]
            parts.append(cb_ref[:, base:base + ck]
                         + u_ref[HALO - 1:HALO - 1 + tm, :] * cw[0:1]
                         + u_ref[HALO:HALO + tm, :] * cw[1:2]
                         + u_ref[HALO + 1:HALO + 1 + tm, :] * cw[2:3])
        gate, val = parts
        g = (gate / (1.0 + jnp.exp(-gate)) * val).astype(BF16)
        yc = _dot(g, wdn_ref[c * ck:(c + 1) * ck, :])
        y = yc if y is None else y + yc
    out_ref[0] = _layer_norm(alpha * x1[HALO:HALO + tm] + y, ln2g_ref[...], ln2b_ref[...])


def _ffn_call(x, o_a, o_b, lne_g, lne_b, on_g, w_o, ln1_g, ln1_b, w_up, conv_w, conv_b, w_down,
              ln2_g, ln2_b, *, tm, alpha, n_chunks):
    B, S, D = x.shape
    d_ff = w_down.shape[0]
    wa, wb = o_a.shape[-1], o_b.shape[-1]
    nt = S // tm
    r8, r16 = tm // HALO, tm // BF16_ROWS
    const = lambda *shape: pl.BlockSpec(shape, lambda b, i: (0,) * len(shape), pipeline_mode=pl.Buffered(1))

    def main(w):
        return pl.BlockSpec((1, tm, w), lambda b, i: (b, i, 0))

    def prev(w, rb, r):
        return pl.BlockSpec((1, rb, w), lambda b, i: (b, jnp.maximum(i * r - 1, 0), 0))

    def nxt(w, rb, r):
        return pl.BlockSpec((1, rb, w), lambda b, i: (b, jnp.minimum((i + 1) * r, S // rb - 1), 0))

    return pl.pallas_call(
        functools.partial(_ffn_kernel, alpha=alpha, d_ff=d_ff, n_chunks=n_chunks),
        out_shape=jax.ShapeDtypeStruct((B, S, D), F32),
        grid=(B, nt),
        in_specs=[
            main(D), prev(D, HALO, r8), nxt(D, HALO, r8),
            main(wa), prev(wa, BF16_ROWS, r16), nxt(wa, BF16_ROWS, r16),
            main(wb), prev(wb, BF16_ROWS, r16), nxt(wb, BF16_ROWS, r16),
            const(*lne_g.shape), const(*lne_b.shape), const(*on_g.shape), const(*w_o.shape),
            const(*ln1_g.shape), const(*ln1_b.shape), const(*w_up.shape), const(*conv_w.shape),
            const(*conv_b.shape), const(*w_down.shape), const(*ln2_g.shape), const(*ln2_b.shape),
        ],
        out_specs=pl.BlockSpec((1, tm, D), lambda b, i: (b, i, 0)),
        scratch_shapes=[pltpu.VMEM((tm + 2 * HALO, d_ff // n_chunks), F32)],
        compiler_params=pltpu.CompilerParams(
            dimension_semantics=("parallel", "parallel"), vmem_limit_bytes=VMEM_LIMIT),
        name="ffn",
    )(x, x, x, o_a, o_a, o_a, o_b, o_b, o_b, lne_g, lne_b, on_g, w_o, ln1_g, ln1_b, w_up, conv_w, conv_b,
      w_down, ln2_g, ln2_b)


def _layer(x, positions, ln_emb_g, ln_emb_b, w_in, q_norm_g, w_uq, kv_norm_g, w_ukv, out_norm_g, w_o,
           ln1_g, ln1_b, w_up, conv_w, conv_b, w_down, ln2_g, ln2_b, *, alpha, tm_proj, tq, tm_ffn, n_chunks):
    B, S, D = x.shape
    row = lambda v: v.reshape(1, -1).astype(F32)
    w_in_p = jnp.concatenate([w_in, jnp.zeros((D, LANES - MLA_ROPE_DIM), w_in.dtype)], axis=1).astype(BF16)
    q_rank = w_uq.shape[0]
    wq = w_uq.reshape(q_rank, MLA_HEADS, MLA_QK_DIM)
    wq = jnp.pad(wq, ((0, 0), (0, 0), (0, MLA_QK_PAD - MLA_QK_DIM)))
    wuqt = wq.reshape(q_rank, MLA_HEADS * MLA_QK_PAD).T.astype(BF16)
    kv_rank = w_ukv.shape[0]
    wkv = w_ukv.reshape(kv_rank, MLA_HEADS, MLA_NOPE_DIM + MLA_V_DIM)
    wuk = wkv[:, :, :MLA_NOPE_DIM].reshape(kv_rank, MLA_HEADS * MLA_NOPE_DIM).astype(BF16)
    wuvt = wkv[:, :, MLA_NOPE_DIM:].reshape(kv_rank, MLA_HEADS * MLA_V_DIM).T.astype(BF16)

    def inv_freq(rot_dim):
        half = rot_dim // 2
        return jnp.power(jnp.float32(ROPE_THETA),
                         -jnp.arange(half, dtype=F32) * (2.0 / rot_dim)).reshape(half, 1)

    pos_row = positions.astype(F32).reshape(B, 1, S)
    q_a, k_a, v_a, qt, kf, vt = _proj_call(
        x, pos_row, inv_freq(SWA_ROT_DIM), inv_freq(MLA_ROPE_DIM), row(ln_emb_g), row(ln_emb_b), w_in_p,
        row(q_norm_g), row(kv_norm_g), wuqt, wuk, wuvt, tm=tm_proj)
    o_a = _swa_call(q_a, k_a, v_a)
    o_b = _mla_call(qt, kf, vt, tq=tq)
    return _ffn_call(x, o_a, o_b, row(ln_emb_g), row(ln_emb_b), row(out_norm_g), w_o.astype(BF16),
                     row(ln1_g), row(ln1_b), w_up.astype(BF16), conv_w.astype(F32), row(conv_b),
                     w_down.astype(BF16), row(ln2_g), row(ln2_b), tm=tm_ffn, alpha=alpha, n_chunks=n_chunks)


def kernel(x, positions, ln_emb_g, ln_emb_b, w_in, q_norm_g, w_uq, kv_norm_g, w_ukv, out_norm_g, w_o,
           ln1_g, ln1_b, w_up, conv_w, conv_b, w_down, ln2_g, ln2_b):
    depth = w_in.shape[0]
    assert depth == 1, "single-layer stack"
    alpha = (2.0 * depth) ** 0.25
    S = x.shape[1]
    tile = min(512, S)
    return _layer(x, positions, ln_emb_g, ln_emb_b, w_in[0], q_norm_g[0], w_uq[0], kv_norm_g[0], w_ukv[0],
                  out_norm_g[0], w_o[0], ln1_g[0], ln1_b[0], w_up[0], conv_w[0], conv_b[0], w_down[0],
                  ln2_g[0], ln2_b[0], alpha=alpha, tm_proj=tile, tq=tile, tm_ffn=tile, n_chunks=2)
```

```python
import functools
import math

import jax
import jax.numpy as jnp
from jax import lax
from jax.experimental import pallas as pl
from jax.experimental.pallas import tpu as pltpu

F32 = jnp.float32
BF16 = jnp.bfloat16

SWA_HEAD_DIM = 64
SWA_PATTERNS = ((128, 1), (512, 4), (2048, 16))
SWA_ROT_DIM = 16
MLA_NOPE_DIM = 128
MLA_ROPE_DIM = 64
MLA_V_DIM = 128
MLA_QK_DIM = MLA_NOPE_DIM + MLA_ROPE_DIM
MLA_HEADS = 4
MLA_QK_PAD = 256
ROPE_THETA = 500000.0
LN_EPS = 1e-5
RMS_EPS = 1e-6
NEG_INF = -1e30
CONV_WIDTH = 3
LOG2E = math.log2(math.e)

LANES = 128
HALO = 8
BF16_ROWS = 16
MLA_VT_ROWS = MLA_V_DIM + BF16_ROWS
VMEM_LIMIT = 56 * 1024 * 1024


def _dot(a, b):
    return jnp.dot(a, b, preferred_element_type=F32)


def _dot_nt(a, b):
    return lax.dot_general(a, b, (((1,), (1,)), ((), ())), preferred_element_type=F32)


def _layer_norm(x, g, b):
    mu = jnp.mean(x, axis=-1, keepdims=True)
    xc = x - mu
    var = jnp.mean(xc * xc, axis=-1, keepdims=True)
    return xc * lax.rsqrt(var + LN_EPS) * g + b


def _rms_norm(x, g):
    return x * lax.rsqrt(jnp.mean(x * x, axis=-1, keepdims=True) + RMS_EPS) * g


def _proj_kernel(x_ref, pos_ref, invfa_ref, invfm_ref, lng_ref, lnb_ref, win_ref, qng_ref, kvng_ref,
                 wuqt_ref, wuk_ref, wuvt_ref,
                 qa_ref, ka_ref, va_ref, qt_ref, kf_ref, vt_ref, *, swa_w, q_rank, kv_rank):
    tm = x_ref.shape[1]
    xn = _layer_norm(x_ref[0], lng_ref[...], lnb_ref[...])
    h = _dot(xn.astype(BF16), win_ref[...])

    pos = pos_ref[0]
    ang_a = invfa_ref[...] * pos
    cos_a, sin_a = jnp.cos(ang_a), jnp.sin(ang_a)
    half = SWA_ROT_DIM // 2
    rest = SWA_HEAD_DIM - SWA_ROT_DIM
    ones = jnp.ones((rest, tm), F32)
    zeros = jnp.zeros((rest, tm), F32)
    zhalf = jnp.zeros((half, tm), F32)
    reps = LANES // SWA_HEAD_DIM
    cos_t = jnp.concatenate([cos_a, cos_a, ones] * reps, axis=0).T
    sin_lo = jnp.concatenate([-sin_a, zhalf, zeros] * reps, axis=0).T
    sin_hi = jnp.concatenate([zhalf, sin_a, zeros] * reps, axis=0).T

    def rope_a(xs):
        return (xs * cos_t + pltpu.roll(xs, LANES - half, 1) * sin_lo + pltpu.roll(xs, half, 1) * sin_hi)

    qscale = SWA_HEAD_DIM ** -0.5 * LOG2E
    for j in range(swa_w // LANES):
        c0 = j * LANES
        qa_ref[0, :, c0:c0 + LANES] = (rope_a(h[:, c0:c0 + LANES]) * qscale).astype(BF16)
        ka_ref[0, :, c0:c0 + LANES] = rope_a(h[:, swa_w + c0:swa_w + c0 + LANES]).astype(BF16)
    va_ref[0] = h[:, 2 * swa_w:3 * swa_w].astype(BF16)

    o_cq = 3 * swa_w
    o_ckv = o_cq + q_rank
    o_kr = o_ckv + kv_rank
    cqn = _rms_norm(h[:, o_cq:o_ckv], qng_ref[...]).astype(BF16)
    ckvn = _rms_norm(h[:, o_ckv:o_kr], kvng_ref[...]).astype(BF16)
    ang_m = invfm_ref[...] * pos
    cos_m, sin_m = jnp.cos(ang_m), jnp.sin(ang_m)
    hm = MLA_ROPE_DIM // 2

    def rope_t(x1, x2):
        return x1 * cos_m - x2 * sin_m, x2 * cos_m + x1 * sin_m

    qt = _dot_nt(wuqt_ref[...], cqn) * (MLA_QK_DIM ** -0.5 * LOG2E)
    for hd in range(MLA_HEADS):
        r0 = hd * MLA_QK_PAD
        qt_ref[0, hd, 0:MLA_NOPE_DIM, :] = qt[r0:r0 + MLA_NOPE_DIM].astype(BF16)
        p0 = r0 + MLA_NOPE_DIM
        n1, n2 = rope_t(qt[p0:p0 + hm], qt[p0 + hm:p0 + 2 * hm])
        qt_ref[0, hd, MLA_NOPE_DIM:MLA_NOPE_DIM + hm, :] = n1.astype(BF16)
        qt_ref[0, hd, MLA_NOPE_DIM + hm:MLA_QK_DIM, :] = n2.astype(BF16)
        qt_ref[0, hd, MLA_QK_DIM:MLA_QK_PAD, :] = jnp.zeros((MLA_QK_PAD - MLA_QK_DIM, tm), BF16)

    kr_t = h[:, o_kr:o_kr + LANES].T
    k1, k2 = rope_t(kr_t[0:hm], kr_t[hm:2 * hm])
    kpe = jnp.concatenate([k1, k2, kr_t[2 * hm:]], axis=0).T.astype(BF16)
    kn = _dot(ckvn, wuk_ref[...])
    for hd in range(MLA_HEADS):
        kf_ref[0, hd, :, 0:MLA_NOPE_DIM] = kn[:, hd * MLA_NOPE_DIM:(hd + 1) * MLA_NOPE_DIM].astype(BF16)
        kf_ref[0, hd, :, MLA_NOPE_DIM:MLA_QK_PAD] = kpe
    vt = _dot_nt(wuvt_ref[...], ckvn)
    for hd in range(MLA_HEADS):
        vt_ref[0, 0, hd, 0:MLA_V_DIM, :] = vt[hd * MLA_V_DIM:(hd + 1) * MLA_V_DIM].astype(BF16)
        vt_ref[0, 0, hd, MLA_V_DIM:MLA_VT_ROWS, :] = jnp.ones((MLA_VT_ROWS - MLA_V_DIM, tm), BF16)


def _proj_call(x, pos_row, invf_a, invf_m, ln_g, ln_b, w_in_p, qn_g, kvn_g, wuqt, wuk, wuvt, *, tm):
    B, S, D = x.shape
    swa_w = 512
    q_rank = qn_g.shape[-1]
    kv_rank = kvn_g.shape[-1]
    nt = S // tm
    const = lambda *shape: pl.BlockSpec(shape, lambda b, i: (0,) * len(shape))
    out_shape = (
        jax.ShapeDtypeStruct((B, S, swa_w), BF16),
        jax.ShapeDtypeStruct((B, S, swa_w), BF16),
        jax.ShapeDtypeStruct((B, S, swa_w), BF16),
        jax.ShapeDtypeStruct((B, MLA_HEADS, MLA_QK_PAD, S), BF16),
        jax.ShapeDtypeStruct((B, MLA_HEADS, S, MLA_QK_PAD), BF16),
        jax.ShapeDtypeStruct((B, nt, MLA_HEADS, MLA_VT_ROWS, tm), BF16),
    )
    return pl.pallas_call(
        functools.partial(_proj_kernel, swa_w=swa_w, q_rank=q_rank, kv_rank=kv_rank),
        out_shape=out_shape,
        grid=(B, nt),
        in_specs=[
            pl.BlockSpec((1, tm, D), lambda b, i: (b, i, 0)),
            pl.BlockSpec((1, 1, tm), lambda b, i: (b, 0, i)),
            const(*invf_a.shape), const(*invf_m.shape), const(*ln_g.shape), const(*ln_b.shape),
            const(*w_in_p.shape), const(*qn_g.shape), const(*kvn_g.shape),
            const(*wuqt.shape), const(*wuk.shape), const(*wuvt.shape),
        ],
        out_specs=(
            pl.BlockSpec((1, tm, swa_w), lambda b, i: (b, i, 0)),
            pl.BlockSpec((1, tm, swa_w), lambda b, i: (b, i, 0)),
            pl.BlockSpec((1, tm, swa_w), lambda b, i: (b, i, 0)),
            pl.BlockSpec((1, MLA_HEADS, MLA_QK_PAD, tm), lambda b, i: (b, 0, 0, i)),
            pl.BlockSpec((1, MLA_HEADS, tm, MLA_QK_PAD), lambda b, i: (b, 0, i, 0)),
            pl.BlockSpec((1, 1, MLA_HEADS, MLA_VT_ROWS, tm), lambda b, i: (b, i, 0, 0, 0)),
        ),
        compiler_params=pltpu.CompilerParams(
            dimension_semantics=("parallel", "parallel"), vmem_limit_bytes=VMEM_LIMIT),
        name="proj",
    )(x, pos_row, invf_a, invf_m, ln_g, ln_b, w_in_p, qn_g, kvn_g, wuqt, wuk, wuvt)


def _swa_geometry(S, window, d, bq_max):
    L = S // d
    n_side = window // (2 * d)
    bq = min(bq_max, L)
    win = min(L, bq + 2 * n_side)
    return L, n_side, bq, win, L // bq


def _swa_kernel(q_ref, k_ref, v_ref, o_ref, stage, qd, kd, vd, bias, op, lp, ot, lt, *, S, bq_max, unroll):
    lane = lax.broadcasted_iota(jnp.int32, (1, LANES), 1)
    head0 = lane < SWA_HEAD_DIM

    dils = [d for _, d in SWA_PATTERNS if d != 1]
    for src, dst, lead in ((q_ref, qd, 0), (k_ref, kd, 0), (v_ref, vd, 1)):
        stage[0] = src[0].astype(F32)
        cur, d_prev = 0, 1
        for pi, d in enumerate(dils):
            f, L, Lp = d // d_prev, S // d, S // d_prev
            for rp in range(d_prev):
                for q in range(f):
                    r = q * d_prev + rp
                    t = stage[cur, pl.ds(rp * Lp + q, L, stride=f), :]
                    dst[lead + pi, r * L:(r + 1) * L, 0:LANES] = t.astype(BF16)
                    if pi + 1 < len(dils):
                        stage[1 - cur, r * L:(r + 1) * L, :] = t
            cur, d_prev = 1 - cur, d
    vd[0, :, 0:LANES] = v_ref[0]
    for p in range(len(SWA_PATTERNS)):
        vd[p, :, LANES:2 * LANES] = jnp.ones((S, LANES), BF16)

    for p, (window, d) in enumerate(SWA_PATTERNS):
        L, n_side, bq, win, nb = _swa_geometry(S, window, d, bq_max)
        rel = (lax.broadcasted_iota(jnp.int32, (bq, win), 0) - lax.broadcasted_iota(jnp.int32, (bq, win), 1))
        ws_last = min(max((nb - 1) * bq - n_side, 0), L - win)
        for c, delta in enumerate((0, n_side, (nb - 1) * bq - ws_last)):
            bias[p, c, 0:bq, 0:win] = jnp.where(jnp.abs(rel + delta) <= n_side, 0.0, NEG_INF)

    pi = 0
    for p, (window, d) in enumerate(SWA_PATTERNS):
        L, n_side, bq, win, nb = _swa_geometry(S, window, d, bq_max)
        if d == 1:
            load_q = lambda r0, n: q_ref[0, pl.ds(r0, n), :]
            load_k = lambda r0, n: k_ref[0, pl.ds(r0, n), :]
        else:
            load_q = functools.partial(lambda r0, n, j: qd[j, pl.ds(r0, n), :], j=pi)
            load_k = functools.partial(lambda r0, n, j: kd[j, pl.ds(r0, n), :], j=pi)
            pi += 1

        def body(n, carry, L=L, n_side=n_side, bq=bq, win=win, nb=nb, p=p, load_q=load_q, load_k=load_k):
            r = n // nb
            i = n % nb
            row0 = pl.multiple_of(r * L + i * bq, bq)
            ws = jnp.clip(i * bq - n_side, 0, L - win)
            krow0 = pl.multiple_of(r * L + ws, BF16_ROWS)
            q = load_q(row0, bq)
            k = load_k(krow0, win)
            v1 = vd[p, pl.ds(krow0, win), :]
            mask = bias[p, jnp.where(i == 0, 0, jnp.where(i == nb - 1, 2, 1)), 0:bq, 0:win]
            outs, lses = [], []
            for hsel in (head0, jnp.logical_not(head0)):
                qh = jnp.where(hsel, q, jnp.zeros_like(q))
                s = _dot_nt(qh, k) + mask
                m = jnp.max(s, axis=1, keepdims=True)
                e = jnp.exp2(s - m).astype(BF16)
                oa = _dot(e, v1)
                den = oa[:, LANES:]
                outs.append(oa[:, :LANES] / den)
                lses.append(m + jnp.log2(den))
            op[p, pl.ds(row0, bq), :] = jnp.where(head0, outs[0], outs[1])
            lp[p, pl.ds(row0, bq), :] = jnp.where(head0, lses[0], lses[1])
            return carry

        lax.fori_loop(0, S // bq, body, 0, unroll=unroll)

    def reader(ref, lead):
        return lambda rows: ref[lead, rows, :]

    def writer(ref, lead):
        def write(rows, val):
            ref[lead, rows, :] = val
        return write

    def spread(read, write, d, d_prev):
        f, L, Lp = d // d_prev, S // d, S // d_prev
        for rp in range(d_prev):
            for q in range(f):
                r = q * d_prev + rp
                write(pl.ds(rp * Lp + q, L, stride=f), read(slice(r * L, (r + 1) * L)))

    pi = 0
    for p, (_, d) in enumerate(SWA_PATTERNS):
        if d == 1:
            continue
        chain = [1] + dils[:pi + 1]
        for src, out in ((op, ot), (lp, lt)):
            read = reader(src, p)
            for step in range(pi, -1, -1):
                write = writer(out, pi) if step == 0 else writer(stage, step % 2)
                spread(read, write, chain[step + 1], chain[step])
                read = reader(stage, step % 2)
        pi += 1

    cr = min(512, S)

    def combine(c, carry):
        r0 = pl.multiple_of(c * cr, cr)
        os_, ls_ = [], []
        pj = 0
        for p, (_, d) in enumerate(SWA_PATTERNS):
            if d == 1:
                os_.append(op[p, pl.ds(r0, cr), :])
                ls_.append(lp[p, pl.ds(r0, cr), :])
            else:
                os_.append(ot[pj, pl.ds(r0, cr), :])
                ls_.append(lt[pj, pl.ds(r0, cr), :])
                pj += 1
        m = functools.reduce(jnp.maximum, ls_)
        es = [jnp.exp2(l - m) for l in ls_]
        den = functools.reduce(lambda a, b: a + b, es)
        num = functools.reduce(lambda a, b: a + b, [e * o for e, o in zip(es, os_)])
        o_ref[0, pl.ds(r0, cr), :] = (num / den).astype(o_ref.dtype)
        return carry

    lax.fori_loop(0, S // cr, combine, 0)


def _swa_call(q_a, k_a, v_a):
    B, S, W = q_a.shape
    npat = len(SWA_PATTERNS)
    nd = sum(1 for _, d in SWA_PATTERNS if d != 1)
    bq_max = 128
    geo = [_swa_geometry(S, w, d, bq_max) for w, d in SWA_PATTERNS]
    bq, win = max(g[2] for g in geo), max(g[3] for g in geo)
    spec = pl.BlockSpec((1, S, LANES), lambda b, j: (b, 0, j))
    return pl.pallas_call(
        functools.partial(_swa_kernel, S=S, bq_max=bq_max, unroll=8),
        out_shape=jax.ShapeDtypeStruct((B, S, W), BF16),
        grid=(B, W // LANES),
        in_specs=[spec, spec, spec],
        out_specs=spec,
        scratch_shapes=[
            pltpu.VMEM((2, S, LANES), F32),
            pltpu.VMEM((nd, S, LANES), BF16),
            pltpu.VMEM((nd, S, LANES), BF16),
            pltpu.VMEM((npat, S, 2 * LANES), BF16),
            pltpu.VMEM((npat, 3, bq, win), F32),
            pltpu.VMEM((npat, S, LANES), F32),
            pltpu.VMEM((npat, S, LANES), F32),
            pltpu.VMEM((nd, S, LANES), F32),
            pltpu.VMEM((nd, S, LANES), F32),
        ],
        compiler_params=pltpu.CompilerParams(
            dimension_semantics=("parallel", "parallel"), vmem_limit_bytes=VMEM_LIMIT),
        name="swa",
    )(q_a, k_a, v_a)


def _mla_kernel(qt_ref, k_ref, vt_ref, o_ref, acc_ref):
    qt = qt_ref[0, 0]
    tq = qt.shape[1]
    nk, tk = vt_ref.shape[1], vt_ref.shape[4]
    acc_ref[...] = jnp.zeros_like(acc_ref)

    def scores(j):
        return _dot(k_ref[0, 0, j * tk:(j + 1) * tk, :], qt)

    st = scores(0)
    m = jnp.full((1, tq), NEG_INF, F32)
    for j in range(nk):
        st_next = scores(j + 1) if j + 1 < nk else None
        m_new = jnp.maximum(m, jnp.max(st, axis=0, keepdims=True))
        a = jnp.exp2(m - m_new)
        e = jnp.exp2(st - m_new).astype(BF16)
        acc_ref[...] = a * acc_ref[...] + _dot(vt_ref[0, j, 0], e)
        st, m = st_next, m_new
    acc = acc_ref[...]
    o_ref[0] = (acc[:MLA_V_DIM] / acc[MLA_V_DIM:MLA_V_DIM + 1]).T.astype(o_ref.dtype)


def _mla_call(qt, kf, vt, *, tq):
    B, H, QP, S = qt.shape
    nk, tk = vt.shape[1], vt.shape[4]
    return pl.pallas_call(
        _mla_kernel,
        out_shape=jax.ShapeDtypeStruct((B, S, H * MLA_V_DIM), BF16),
        grid=(B, H, S // tq),
        in_specs=[
            pl.BlockSpec((1, 1, QP, tq), lambda b, h, i: (b, h, 0, i)),
            pl.BlockSpec((1, 1, S, QP), lambda b, h, i: (b, h, 0, 0)),
            pl.BlockSpec((1, nk, 1, MLA_VT_ROWS, tk), lambda b, h, i: (b, 0, h, 0, 0)),
        ],
        out_specs=pl.BlockSpec((1, tq, MLA_V_DIM), lambda b, h, i: (b, i, h)),
        scratch_shapes=[pltpu.VMEM((MLA_VT_ROWS, tq), F32)],
        compiler_params=pltpu.CompilerParams(
            dimension_semantics=("parallel", "parallel", "parallel"), vmem_limit_bytes=VMEM_LIMIT),
        name="mla",
    )(qt, kf, vt)


def _ffn_kernel(x_ref, xp_ref, xn_ref, oa_ref, oap_ref, oan_ref, ob_ref, obp_ref, obn_ref,
                lneg_ref, lneb_ref, ong_ref, wo_ref, ln1g_ref, ln1b_ref, wup_ref, cw_ref, cb_ref,
                wdn_ref, ln2g_ref, ln2b_ref, out_ref, st, u_ref, *, alpha, d_ff, n_chunks):
    tm = x_ref.shape[1]
    rows = tm + 2 * HALO
    G = rows // HALO
    i = pl.program_id(1)
    last = pl.num_programs(1) - 1
    hi = BF16_ROWS - HALO
    nslab = x_ref.shape[2] // LANES

    def with_halo(main, prev, nxt):
        return jnp.concatenate([prev, main, nxt], axis=0)

    x = with_halo(x_ref[0], xp_ref[0], xn_ref[0])
    oa = with_halo(oa_ref[0].astype(F32), oap_ref[0].astype(F32)[hi:], oan_ref[0].astype(F32)[:HALO])
    ob = with_halo(ob_ref[0].astype(F32), obp_ref[0].astype(F32)[hi:], obn_ref[0].astype(F32)[:HALO])
    wa = oa.shape[1]
    xn = _layer_norm(x, lneg_ref[...], lneb_ref[...])
    o = jnp.concatenate([_rms_norm(oa, ong_ref[:, :wa]), _rms_norm(ob, ong_ref[:, wa:])], axis=1)
    x1 = _layer_norm(alpha * xn + _dot(o.astype(BF16), wo_ref[...]), ln1g_ref[...], ln1b_ref[...])

    ridx = lax.broadcasted_iota(jnp.int32, (rows, 1), 0)
    inside = jnp.logical_and(jnp.logical_or(ridx >= HALO, i > 0),
                             jnp.logical_or(ridx < tm + HALO, i < last))
    x1m = jnp.where(inside, x1, 0.0)

    for j in range(nslab):
        st[j] = x1m[:, j * LANES:(j + 1) * LANES]
    x1p = jnp.concatenate(
        [jnp.concatenate([st[j, pl.ds(g, HALO, stride=G), :] for j in range(nslab)], axis=1) for g in range(G)],
        axis=0).astype(BF16)

    ck = d_ff // n_chunks
    y = None
    for c in range(n_chunks):
        parts = []
        for base in (c * ck, d_ff + c * ck):
            u = _dot(x1p, wup_ref[:, base:base + ck])
            u_ref[HALO:HALO + rows, :] = u
            u_ref[0:HALO, :] = pltpu.roll(u[rows - HALO:rows], 1, 0)
            u_ref[HALO + rows:2 * HALO + rows, :] = pltpu.roll(u[0:HALO], HALO - 1, 0)
            cw = cw_ref[:, base:base + ck]
            parts.append(cb_ref[:, base:base + ck]
                         + u_ref[0:rows, :] * cw[0:1]
                         + u_ref[HALO:HALO + rows, :] * cw[1:2]
                         + u_ref[2 * HALO:2 * HALO + rows, :] * cw[2:3])
        gate, val = parts
        g = (gate / (1.0 + jnp.exp2(gate * (-LOG2E))) * val).astype(BF16)
        yc = _dot(g, wdn_ref[c * ck:(c + 1) * ck, :])
        y = yc if y is None else y + yc
    for g in range(G):
        for j in range(nslab):
            st[j, pl.ds(g, HALO, stride=G), :] = y[g * HALO:(g + 1) * HALO, j * LANES:(j + 1) * LANES]
    y = jnp.concatenate([st[j, HALO:HALO + tm, :] for j in range(nslab)], axis=1)
    out_ref[0] = _layer_norm(alpha * x1[HALO:HALO + tm] + y, ln2g_ref[...], ln2b_ref[...])


def _ffn_call(x, o_a, o_b, lne_g, lne_b, on_g, w_o, ln1_g, ln1_b, w_up, conv_w, conv_b, w_down,
              ln2_g, ln2_b, *, tm, alpha, n_chunks):
    B, S, D = x.shape
    d_ff = w_down.shape[0]
    wa, wb = o_a.shape[-1], o_b.shape[-1]
    nt = S // tm
    r8, r16 = tm // HALO, tm // BF16_ROWS
    const = lambda *shape: pl.BlockSpec(shape, lambda b, i: (0,) * len(shape), pipeline_mode=pl.Buffered(1))

    def main(w):
        return pl.BlockSpec((1, tm, w), lambda b, i: (b, i, 0))

    def prev(w, rb, r):
        return pl.BlockSpec((1, rb, w), lambda b, i: (b, jnp.maximum(i * r - 1, 0), 0))

    def nxt(w, rb, r):
        return pl.BlockSpec((1, rb, w), lambda b, i: (b, jnp.minimum((i + 1) * r, S // rb - 1), 0))

    return pl.pallas_call(
        functools.partial(_ffn_kernel, alpha=alpha, d_ff=d_ff, n_chunks=n_chunks),
        out_shape=jax.ShapeDtypeStruct((B, S, D), F32),
        grid=(B, nt),
        in_specs=[
            main(D), prev(D, HALO, r8), nxt(D, HALO, r8),
            main(wa), prev(wa, BF16_ROWS, r16), nxt(wa, BF16_ROWS, r16),
            main(wb), prev(wb, BF16_ROWS, r16), nxt(wb, BF16_ROWS, r16),
            const(*lne_g.shape), const(*lne_b.shape), const(*on_g.shape), const(*w_o.shape),
            const(*ln1_g.shape), const(*ln1_b.shape), const(*w_up.shape), const(*conv_w.shape),
            const(*conv_b.shape), const(*w_down.shape), const(*ln2_g.shape), const(*ln2_b.shape),
        ],
        out_specs=pl.BlockSpec((1, tm, D), lambda b, i: (b, i, 0)),
        scratch_shapes=[pltpu.VMEM((D // LANES, tm + 2 * HALO, LANES), F32),
                        pltpu.VMEM((tm + 4 * HALO, d_ff // n_chunks), F32)],
        compiler_params=pltpu.CompilerParams(
            dimension_semantics=("parallel", "parallel"), vmem_limit_bytes=VMEM_LIMIT),
        name="ffn",
    )(x, x, x, o_a, o_a, o_a, o_b, o_b, o_b, lne_g, lne_b, on_g, w_o, ln1_g, ln1_b, w_up, conv_w, conv_b,
      w_down, ln2_g, ln2_b)


def _layer(x, positions, ln_emb_g, ln_emb_b, w_in, q_norm_g, w_uq, kv_norm_g, w_ukv, out_norm_g, w_o,
           ln1_g, ln1_b, w_up, conv_w, conv_b, w_down, ln2_g, ln2_b, *, alpha, tm_proj, tq, tm_ffn, n_chunks):
    B, S, D = x.shape
    row = lambda v: v.reshape(1, -1).astype(F32)
    w_in_p = jnp.concatenate([w_in, jnp.zeros((D, LANES - MLA_ROPE_DIM), w_in.dtype)], axis=1).astype(BF16)
    q_rank = w_uq.shape[0]
    wq = w_uq.reshape(q_rank, MLA_HEADS, MLA_QK_DIM)
    wq = jnp.pad(wq, ((0, 0), (0, 0), (0, MLA_QK_PAD - MLA_QK_DIM)))
    wuqt = wq.reshape(q_rank, MLA_HEADS * MLA_QK_PAD).T.astype(BF16)
    kv_rank = w_ukv.shape[0]
    wkv = w_ukv.reshape(kv_rank, MLA_HEADS, MLA_NOPE_DIM + MLA_V_DIM)
    wuk = wkv[:, :, :MLA_NOPE_DIM].reshape(kv_rank, MLA_HEADS * MLA_NOPE_DIM).astype(BF16)
    wuvt = wkv[:, :, MLA_NOPE_DIM:].reshape(kv_rank, MLA_HEADS * MLA_V_DIM).T.astype(BF16)

    def inv_freq(rot_dim):
        half = rot_dim // 2
        return jnp.power(jnp.float32(ROPE_THETA),
                         -jnp.arange(half, dtype=F32) * (2.0 / rot_dim)).reshape(half, 1)

    pos_row = positions.astype(F32).reshape(B, 1, S)
    q_a, k_a, v_a, qt, kf, vt = _proj_call(
        x, pos_row, inv_freq(SWA_ROT_DIM), inv_freq(MLA_ROPE_DIM), row(ln_emb_g), row(ln_emb_b), w_in_p,
        row(q_norm_g), row(kv_norm_g), wuqt, wuk, wuvt, tm=tm_proj)
    o_a = _swa_call(q_a, k_a, v_a)
    o_b = _mla_call(qt, kf, vt, tq=tq)
    return _ffn_call(x, o_a, o_b, row(ln_emb_g), row(ln_emb_b), row(out_norm_g), w_o.astype(BF16),
                     row(ln1_g), row(ln1_b), w_up.astype(BF16), conv_w.astype(F32), row(conv_b),
                     w_down.astype(BF16), row(ln2_g), row(ln2_b), tm=tm_ffn, alpha=alpha, n_chunks=n_chunks)


def kernel(x, positions, ln_emb_g, ln_emb_b, w_in, q_norm_g, w_uq, kv_norm_g, w_ukv, out_norm_g, w_o,
           ln1_g, ln1_b, w_up, conv_w, conv_b, w_down, ln2_g, ln2_b):
    depth = w_in.shape[0]
    assert depth == 1, "single-layer stack"
    alpha = (2.0 * depth) ** 0.25
    S = x.shape[1]
    tile = min(512, S)
    return _layer(x, positions, ln_emb_g, ln_emb_b, w_in[0], q_norm_g[0], w_uq[0], kv_norm_g[0], w_ukv[0],
                  out_norm_g[0], w_o[0], ln1_g[0], ln1_b[0], w_up[0], conv_w[0], conv_b[0], w_down[0],
                  ln2_g[0], ln2_b[0], alpha=alpha, tm_proj=tile, tq=tile, tm_ffn=tile, n_chunks=2)
```

```python
import functools
import math

import jax
import jax.numpy as jnp
from jax import lax
from jax.experimental import pallas as pl
from jax.experimental.pallas import tpu as pltpu

F32 = jnp.float32
BF16 = jnp.bfloat16

SWA_HEAD_DIM = 64
SWA_PATTERNS = ((128, 1), (512, 4), (2048, 16))
SWA_ROT_DIM = 16
MLA_NOPE_DIM = 128
MLA_ROPE_DIM = 64
MLA_V_DIM = 128
MLA_QK_DIM = MLA_NOPE_DIM + MLA_ROPE_DIM
MLA_HEADS = 4
MLA_QK_PAD = 256
ROPE_THETA = 500000.0
LN_EPS = 1e-5
RMS_EPS = 1e-6
NEG_INF = -1e30
CONV_WIDTH = 3
LOG2E = math.log2(math.e)

LANES = 128
HALO = 8
BF16_ROWS = 16
MLA_VT_ROWS = MLA_V_DIM + BF16_ROWS
MLA_JUMP_LIMIT = 60.0
VMEM_LIMIT = 56 * 1024 * 1024


def _dot(a, b):
    return jnp.dot(a, b, preferred_element_type=F32)


def _dot_nt(a, b):
    return lax.dot_general(a, b, (((1,), (1,)), ((), ())), preferred_element_type=F32)


def _layer_norm(x, g, b):
    mu = jnp.mean(x, axis=-1, keepdims=True)
    xc = x - mu
    var = jnp.mean(xc * xc, axis=-1, keepdims=True)
    return xc * lax.rsqrt(var + LN_EPS) * g + b


def _rms_norm(x, g):
    return x * lax.rsqrt(jnp.mean(x * x, axis=-1, keepdims=True) + RMS_EPS) * g


def _proj_kernel(x_ref, pos_ref, invfa_ref, invfm_ref, lng_ref, lnb_ref, win_ref, qng_ref, kvng_ref,
                 wuqt_ref, wuk_ref, wuvt_ref,
                 qa_ref, ka_ref, va_ref, qt_ref, kf_ref, vt_ref, *, swa_w, q_rank, kv_rank):
    tm = x_ref.shape[1]
    xn = _layer_norm(x_ref[0], lng_ref[...], lnb_ref[...])
    h = _dot(xn.astype(BF16), win_ref[...])

    pos = pos_ref[0]
    ang_a = invfa_ref[...] * pos
    cos_a, sin_a = jnp.cos(ang_a), jnp.sin(ang_a)
    half = SWA_ROT_DIM // 2
    rest = SWA_HEAD_DIM - SWA_ROT_DIM
    ones = jnp.ones((rest, tm), F32)
    zeros = jnp.zeros((rest, tm), F32)
    zhalf = jnp.zeros((half, tm), F32)
    reps = LANES // SWA_HEAD_DIM
    cos_t = jnp.concatenate([cos_a, cos_a, ones] * reps, axis=0).T
    sin_lo = jnp.concatenate([-sin_a, zhalf, zeros] * reps, axis=0).T
    sin_hi = jnp.concatenate([zhalf, sin_a, zeros] * reps, axis=0).T

    def rope_a(xs):
        return (xs * cos_t + pltpu.roll(xs, LANES - half, 1) * sin_lo + pltpu.roll(xs, half, 1) * sin_hi)

    qscale = SWA_HEAD_DIM ** -0.5 * LOG2E
    for j in range(swa_w // LANES):
        c0 = j * LANES
        qa_ref[0, :, c0:c0 + LANES] = (rope_a(h[:, c0:c0 + LANES]) * qscale).astype(BF16)
        ka_ref[0, :, c0:c0 + LANES] = rope_a(h[:, swa_w + c0:swa_w + c0 + LANES]).astype(BF16)
    va_ref[0] = h[:, 2 * swa_w:3 * swa_w].astype(BF16)

    o_cq = 3 * swa_w
    o_ckv = o_cq + q_rank
    o_kr = o_ckv + kv_rank
    cqn = _rms_norm(h[:, o_cq:o_ckv], qng_ref[...]).astype(BF16)
    ckvn = _rms_norm(h[:, o_ckv:o_kr], kvng_ref[...]).astype(BF16)
    ang_m = invfm_ref[...] * pos
    cos_m, sin_m = jnp.cos(ang_m), jnp.sin(ang_m)
    hm = MLA_ROPE_DIM // 2

    def rope_t(x1, x2):
        return x1 * cos_m - x2 * sin_m, x2 * cos_m + x1 * sin_m

    qt = _dot_nt(wuqt_ref[...], cqn) * (MLA_QK_DIM ** -0.5 * LOG2E)
    for hd in range(MLA_HEADS):
        r0 = hd * MLA_QK_PAD
        qt_ref[0, hd, 0:MLA_NOPE_DIM, :] = qt[r0:r0 + MLA_NOPE_DIM].astype(BF16)
        p0 = r0 + MLA_NOPE_DIM
        n1, n2 = rope_t(qt[p0:p0 + hm], qt[p0 + hm:p0 + 2 * hm])
        qt_ref[0, hd, MLA_NOPE_DIM:MLA_NOPE_DIM + hm, :] = n1.astype(BF16)
        qt_ref[0, hd, MLA_NOPE_DIM + hm:MLA_QK_DIM, :] = n2.astype(BF16)
        qt_ref[0, hd, MLA_QK_DIM:MLA_QK_PAD, :] = jnp.zeros((MLA_QK_PAD - MLA_QK_DIM, tm), BF16)

    kr_t = h[:, o_kr:o_kr + LANES].T
    k1, k2 = rope_t(kr_t[0:hm], kr_t[hm:2 * hm])
    kpe = jnp.concatenate([k1, k2, kr_t[2 * hm:]], axis=0).T.astype(BF16)
    kn = _dot(ckvn, wuk_ref[...])
    for hd in range(MLA_HEADS):
        kf_ref[0, hd, :, 0:MLA_NOPE_DIM] = kn[:, hd * MLA_NOPE_DIM:(hd + 1) * MLA_NOPE_DIM].astype(BF16)
        kf_ref[0, hd, :, MLA_NOPE_DIM:MLA_QK_PAD] = kpe
    vt = _dot_nt(wuvt_ref[...], ckvn)
    for hd in range(MLA_HEADS):
        vt_ref[0, 0, hd, 0:MLA_V_DIM, :] = vt[hd * MLA_V_DIM:(hd + 1) * MLA_V_DIM].astype(BF16)
        vt_ref[0, 0, hd, MLA_V_DIM:MLA_VT_ROWS, :] = jnp.ones((MLA_VT_ROWS - MLA_V_DIM, tm), BF16)


def _proj_call(x, pos_row, invf_a, invf_m, ln_g, ln_b, w_in_p, qn_g, kvn_g, wuqt, wuk, wuvt, *, tm):
    B, S, D = x.shape
    swa_w = 512
    q_rank = qn_g.shape[-1]
    kv_rank = kvn_g.shape[-1]
    nt = S // tm
    const = lambda *shape: pl.BlockSpec(shape, lambda b, i: (0,) * len(shape))
    out_shape = (
        jax.ShapeDtypeStruct((B, S, swa_w), BF16),
        jax.ShapeDtypeStruct((B, S, swa_w), BF16),
        jax.ShapeDtypeStruct((B, S, swa_w), BF16),
        jax.ShapeDtypeStruct((B, MLA_HEADS, MLA_QK_PAD, S), BF16),
        jax.ShapeDtypeStruct((B, MLA_HEADS, S, MLA_QK_PAD), BF16),
        jax.ShapeDtypeStruct((B, nt, MLA_HEADS, MLA_VT_ROWS, tm), BF16),
    )
    return pl.pallas_call(
        functools.partial(_proj_kernel, swa_w=swa_w, q_rank=q_rank, kv_rank=kv_rank),
        out_shape=out_shape,
        grid=(B, nt),
        in_specs=[
            pl.BlockSpec((1, tm, D), lambda b, i: (b, i, 0)),
            pl.BlockSpec((1, 1, tm), lambda b, i: (b, 0, i)),
            const(*invf_a.shape), const(*invf_m.shape), const(*ln_g.shape), const(*ln_b.shape),
            const(*w_in_p.shape), const(*qn_g.shape), const(*kvn_g.shape),
            const(*wuqt.shape), const(*wuk.shape), const(*wuvt.shape),
        ],
        out_specs=(
            pl.BlockSpec((1, tm, swa_w), lambda b, i: (b, i, 0)),
            pl.BlockSpec((1, tm, swa_w), lambda b, i: (b, i, 0)),
            pl.BlockSpec((1, tm, swa_w), lambda b, i: (b, i, 0)),
            pl.BlockSpec((1, MLA_HEADS, MLA_QK_PAD, tm), lambda b, i: (b, 0, 0, i)),
            pl.BlockSpec((1, MLA_HEADS, tm, MLA_QK_PAD), lambda b, i: (b, 0, i, 0)),
            pl.BlockSpec((1, 1, MLA_HEADS, MLA_VT_ROWS, tm), lambda b, i: (b, i, 0, 0, 0)),
        ),
        compiler_params=pltpu.CompilerParams(
            dimension_semantics=("parallel", "parallel"), vmem_limit_bytes=VMEM_LIMIT),
        name="proj",
    )(x, pos_row, invf_a, invf_m, ln_g, ln_b, w_in_p, qn_g, kvn_g, wuqt, wuk, wuvt)


def _swa_geometry(S, window, d, bq_max):
    L = S // d
    n_side = window // (2 * d)
    bq = min(bq_max, L)
    win = min(L, bq + 2 * n_side)
    return L, n_side, bq, win, L // bq


def _swa_kernel(q_ref, k_ref, v_ref, o_ref, stage, qd, kd, vd, bias, op, lp, ot, lt, *, S, bq_max, unroll):
    lane = lax.broadcasted_iota(jnp.int32, (1, LANES), 1)
    head0 = lane < SWA_HEAD_DIM

    dils = [d for _, d in SWA_PATTERNS if d != 1]
    for src, dst, lead in ((q_ref, qd, 0), (k_ref, kd, 0), (v_ref, vd, 1)):
        stage[0] = src[0].astype(F32)
        cur, d_prev = 0, 1
        for pi, d in enumerate(dils):
            f, L, Lp = d // d_prev, S // d, S // d_prev
            for rp in range(d_prev):
                for q in range(f):
                    r = q * d_prev + rp
                    t = stage[cur, pl.ds(rp * Lp + q, L, stride=f), :]
                    dst[lead + pi, r * L:(r + 1) * L, 0:LANES] = t.astype(BF16)
                    if pi + 1 < len(dils):
                        stage[1 - cur, r * L:(r + 1) * L, :] = t
            cur, d_prev = 1 - cur, d
    vd[0, :, 0:LANES] = v_ref[0]
    for p in range(len(SWA_PATTERNS)):
        vd[p, :, LANES:2 * LANES] = jnp.ones((S, LANES), BF16)

    for p, (window, d) in enumerate(SWA_PATTERNS):
        L, n_side, bq, win, nb = _swa_geometry(S, window, d, bq_max)
        rel = (lax.broadcasted_iota(jnp.int32, (bq, win), 0) - lax.broadcasted_iota(jnp.int32, (bq, win), 1))
        ws_last = min(max((nb - 1) * bq - n_side, 0), L - win)
        for c, delta in enumerate((0, n_side, (nb - 1) * bq - ws_last)):
            bias[p, c, 0:bq, 0:win] = jnp.where(jnp.abs(rel + delta) <= n_side, 0.0, NEG_INF)

    pi = 0
    for p, (window, d) in enumerate(SWA_PATTERNS):
        L, n_side, bq, win, nb = _swa_geometry(S, window, d, bq_max)
        if d == 1:
            load_q = lambda r0, n: q_ref[0, pl.ds(r0, n), :]
            load_k = lambda r0, n: k_ref[0, pl.ds(r0, n), :]
        else:
            load_q = functools.partial(lambda r0, n, j: qd[j, pl.ds(r0, n), :], j=pi)
            load_k = functools.partial(lambda r0, n, j: kd[j, pl.ds(r0, n), :], j=pi)
            pi += 1

        def body(n, carry, L=L, n_side=n_side, bq=bq, win=win, nb=nb, p=p, load_q=load_q, load_k=load_k):
            r = n // nb
            i = n % nb
            row0 = pl.multiple_of(r * L + i * bq, bq)
            ws = jnp.clip(i * bq - n_side, 0, L - win)
            krow0 = pl.multiple_of(r * L + ws, BF16_ROWS)
            q = load_q(row0, bq)
            k = load_k(krow0, win)
            v1 = vd[p, pl.ds(krow0, win), :]
            mask = bias[p, jnp.where(i == 0, 0, jnp.where(i == nb - 1, 2, 1)), 0:bq, 0:win]
            outs, lses = [], []
            for hsel in (head0, jnp.logical_not(head0)):
                qh = jnp.where(hsel, q, jnp.zeros_like(q))
                s = _dot_nt(qh, k) + mask
                m = jnp.max(s, axis=1, keepdims=True)
                e = jnp.exp2(s - m).astype(BF16)
                oa = _dot(e, v1)
                den = oa[:, LANES:]
                outs.append(oa[:, :LANES] / den)
                lses.append(m + jnp.log2(den))
            op[p, pl.ds(row0, bq), :] = jnp.where(head0, outs[0], outs[1])
            lp[p, pl.ds(row0, bq), :] = jnp.where(head0, lses[0], lses[1])
            return carry

        lax.fori_loop(0, S // bq, body, 0, unroll=unroll)

    def reader(ref, lead):
        return lambda rows: ref[lead, rows, :]

    def writer(ref, lead):
        def write(rows, val):
            ref[lead, rows, :] = val
        return write

    def spread(read, write, d, d_prev):
        f, L, Lp = d // d_prev, S // d, S // d_prev
        for rp in range(d_prev):
            for q in range(f):
                r = q * d_prev + rp
                write(pl.ds(rp * Lp + q, L, stride=f), read(slice(r * L, (r + 1) * L)))

    pi = 0
    for p, (_, d) in enumerate(SWA_PATTERNS):
        if d == 1:
            continue
        chain = [1] + dils[:pi + 1]
        for src, out in ((op, ot), (lp, lt)):
            read = reader(src, p)
            for step in range(pi, -1, -1):
                write = writer(out, pi) if step == 0 else writer(stage, step % 2)
                spread(read, write, chain[step + 1], chain[step])
                read = reader(stage, step % 2)
        pi += 1

    cr = min(512, S)

    def combine(c, carry):
        r0 = pl.multiple_of(c * cr, cr)
        os_, ls_ = [], []
        pj = 0
        for p, (_, d) in enumerate(SWA_PATTERNS):
            if d == 1:
                os_.append(op[p, pl.ds(r0, cr), :])
                ls_.append(lp[p, pl.ds(r0, cr), :])
            else:
                os_.append(ot[pj, pl.ds(r0, cr), :])
                ls_.append(lt[pj, pl.ds(r0, cr), :])
                pj += 1
        m = functools.reduce(jnp.maximum, ls_)
        es = [jnp.exp2(l - m) for l in ls_]
        den = functools.reduce(lambda a, b: a + b, es)
        num = functools.reduce(lambda a, b: a + b, [e * o for e, o in zip(es, os_)])
        o_ref[0, pl.ds(r0, cr), :] = (num / den).astype(o_ref.dtype)
        return carry

    lax.fori_loop(0, S // cr, combine, 0)


def _swa_call(q_a, k_a, v_a):
    B, S, W = q_a.shape
    npat = len(SWA_PATTERNS)
    nd = sum(1 for _, d in SWA_PATTERNS if d != 1)
    bq_max = 128
    geo = [_swa_geometry(S, w, d, bq_max) for w, d in SWA_PATTERNS]
    bq, win = max(g[2] for g in geo), max(g[3] for g in geo)
    spec = pl.BlockSpec((1, S, LANES), lambda b, j: (b, 0, j))
    return pl.pallas_call(
        functools.partial(_swa_kernel, S=S, bq_max=bq_max, unroll=8),
        out_shape=jax.ShapeDtypeStruct((B, S, W), BF16),
        grid=(B, W // LANES),
        in_specs=[spec, spec, spec],
        out_specs=spec,
        scratch_shapes=[
            pltpu.VMEM((2, S, LANES), F32),
            pltpu.VMEM((nd, S, LANES), BF16),
            pltpu.VMEM((nd, S, LANES), BF16),
            pltpu.VMEM((npat, S, 2 * LANES), BF16),
            pltpu.VMEM((npat, 3, bq, win), F32),
            pltpu.VMEM((npat, S, LANES), F32),
            pltpu.VMEM((npat, S, LANES), F32),
            pltpu.VMEM((nd, S, LANES), F32),
            pltpu.VMEM((nd, S, LANES), F32),
        ],
        compiler_params=pltpu.CompilerParams(
            dimension_semantics=("parallel", "parallel"), vmem_limit_bytes=VMEM_LIMIT),
        name="swa",
    )(q_a, k_a, v_a)


def _mla_kernel(qt_ref, k_ref, vt_ref, o_ref, acc_ref):
    qt = qt_ref[0, 0]
    tq = qt.shape[1]
    nk, tk = vt_ref.shape[1], vt_ref.shape[4]

    def scores(j):
        return _dot(k_ref[0, 0, j * tk:(j + 1) * tk, :], qt)

    def pv(j, e):
        return _dot(vt_ref[0, j, 0], e.astype(BF16))

    def finish():
        acc = acc_ref[...]
        o_ref[0] = (acc[:MLA_V_DIM] / acc[MLA_V_DIM:MLA_V_DIM + 1]).T.astype(o_ref.dtype)

    st = scores(0)
    st_next = scores(1) if nk > 1 else None
    m = jnp.max(st, axis=0, keepdims=True)
    acc_ref[...] = pv(0, jnp.exp2(st - m))
    jump = jnp.zeros((1, tq), F32)
    for j in range(1, nk):
        st, st_next = st_next, (scores(j + 1) if j + 1 < nk else None)
        mt = jnp.max(st, axis=0, keepdims=True)
        acc = acc_ref[...] + pv(j, jnp.exp2(st - m))
        jump = jnp.maximum(jump, mt - m)
        m_new = jnp.maximum(m, mt)
        acc_ref[...] = acc * jnp.exp2(m - m_new)
        m = m_new
    in_range = jnp.max(jump) <= MLA_JUMP_LIMIT

    @pl.when(in_range)
    def _():
        finish()

    @pl.when(jnp.logical_not(in_range))
    def _():
        acc_ref[...] = jnp.zeros_like(acc_ref)
        st = scores(0)
        m = jnp.full((1, tq), NEG_INF, F32)
        for j in range(nk):
            st_next = scores(j + 1) if j + 1 < nk else None
            m_new = jnp.maximum(m, jnp.max(st, axis=0, keepdims=True))
            acc_ref[...] = jnp.exp2(m - m_new) * acc_ref[...] + pv(j, jnp.exp2(st - m_new))
            st, m = st_next, m_new
        finish()


def _mla_call(qt, kf, vt, *, tq):
    B, H, QP, S = qt.shape
    nk, tk = vt.shape[1], vt.shape[4]
    return pl.pallas_call(
        _mla_kernel,
        out_shape=jax.ShapeDtypeStruct((B, S, H * MLA_V_DIM), BF16),
        grid=(B, H, S // tq),
        in_specs=[
            pl.BlockSpec((1, 1, QP, tq), lambda b, h, i: (b, h, 0, i)),
            pl.BlockSpec((1, 1, S, QP), lambda b, h, i: (b, h, 0, 0)),
            pl.BlockSpec((1, nk, 1, MLA_VT_ROWS, tk), lambda b, h, i: (b, 0, h, 0, 0)),
        ],
        out_specs=pl.BlockSpec((1, tq, MLA_V_DIM), lambda b, h, i: (b, i, h)),
        scratch_shapes=[pltpu.VMEM((MLA_VT_ROWS, tq), F32)],
        compiler_params=pltpu.CompilerParams(
            dimension_semantics=("parallel", "parallel", "parallel"), vmem_limit_bytes=VMEM_LIMIT),
        name="mla",
    )(qt, kf, vt)


def _ffn_kernel(x_ref, xp_ref, xn_ref, oa_ref, oap_ref, oan_ref, ob_ref, obp_ref, obn_ref,
                lneg_ref, lneb_ref, ong_ref, wo_ref, ln1g_ref, ln1b_ref, wup_ref, cw_ref, cb_ref,
                wdn_ref, ln2g_ref, ln2b_ref, out_ref, st, u_ref, *, alpha, d_ff, n_chunks):
    tm = x_ref.shape[1]
    rows = tm + 2 * HALO
    G = rows // HALO
    i = pl.program_id(1)
    last = pl.num_programs(1) - 1
    hi = BF16_ROWS - HALO
    nslab = x_ref.shape[2] // LANES

    def with_halo(main, prev, nxt):
        return jnp.concatenate([prev, main, nxt], axis=0)

    x = with_halo(x_ref[0], xp_ref[0], xn_ref[0])
    oa = with_halo(oa_ref[0].astype(F32), oap_ref[0].astype(F32)[hi:], oan_ref[0].astype(F32)[:HALO])
    ob = with_halo(ob_ref[0].astype(F32), obp_ref[0].astype(F32)[hi:], obn_ref[0].astype(F32)[:HALO])
    wa = oa.shape[1]
    xn = _layer_norm(x, lneg_ref[...], lneb_ref[...])
    o = jnp.concatenate([_rms_norm(oa, ong_ref[:, :wa]), _rms_norm(ob, ong_ref[:, wa:])], axis=1)
    x1 = _layer_norm(alpha * xn + _dot(o.astype(BF16), wo_ref[...]), ln1g_ref[...], ln1b_ref[...])

    ridx = lax.broadcasted_iota(jnp.int32, (rows, 1), 0)
    inside = jnp.logical_and(jnp.logical_or(ridx >= HALO, i > 0),
                             jnp.logical_or(ridx < tm + HALO, i < last))
    x1m = jnp.where(inside, x1, 0.0)

    for j in range(nslab):
        st[j] = x1m[:, j * LANES:(j + 1) * LANES]
    x1p = jnp.concatenate(
        [jnp.concatenate([st[j, pl.ds(g, HALO, stride=G), :] for j in range(nslab)], axis=1) for g in range(G)],
        axis=0).astype(BF16)

    ck = d_ff // n_chunks
    y = None
    for c in range(n_chunks):
        parts = []
        for base in (c * ck, d_ff + c * ck):
            u = _dot(x1p, wup_ref[:, base:base + ck])
            u_ref[HALO:HALO + rows, :] = u
            u_ref[0:HALO, :] = pltpu.roll(u[rows - HALO:rows], 1, 0)
            u_ref[HALO + rows:2 * HALO + rows, :] = pltpu.roll(u[0:HALO], HALO - 1, 0)
            cw = cw_ref[:, base:base + ck]
            parts.append(cb_ref[:, base:base + ck]
                         + u_ref[0:rows, :] * cw[0:1]
                         + u_ref[HALO:HALO + rows, :] * cw[1:2]
                         + u_ref[2 * HALO:2 * HALO + rows, :] * cw[2:3])
        gate, val = parts
        g = (gate / (1.0 + jnp.exp2(gate * (-LOG2E))) * val).astype(BF16)
        yc = _dot(g, wdn_ref[c * ck:(c + 1) * ck, :])
        y = yc if y is None else y + yc
    for g in range(G):
        for j in range(nslab):
            st[j, pl.ds(g, HALO, stride=G), :] = y[g * HALO:(g + 1) * HALO, j * LANES:(j + 1) * LANES]
    y = jnp.concatenate([st[j, HALO:HALO + tm, :] for j in range(nslab)], axis=1)
    out_ref[0] = _layer_norm(alpha * x1[HALO:HALO + tm] + y, ln2g_ref[...], ln2b_ref[...])


def _ffn_call(x, o_a, o_b, lne_g, lne_b, on_g, w_o, ln1_g, ln1_b, w_up, conv_w, conv_b, w_down,
              ln2_g, ln2_b, *, tm, alpha, n_chunks):
    B, S, D = x.shape
    d_ff = w_down.shape[0]
    wa, wb = o_a.shape[-1], o_b.shape[-1]
    nt = S // tm
    r8, r16 = tm // HALO, tm // BF16_ROWS
    const = lambda *shape: pl.BlockSpec(shape, lambda b, i: (0,) * len(shape), pipeline_mode=pl.Buffered(1))

    def main(w):
        return pl.BlockSpec((1, tm, w), lambda b, i: (b, i, 0))

    def prev(w, rb, r):
        return pl.BlockSpec((1, rb, w), lambda b, i: (b, jnp.maximum(i * r - 1, 0), 0))

    def nxt(w, rb, r):
        return pl.BlockSpec((1, rb, w), lambda b, i: (b, jnp.minimum((i + 1) * r, S // rb - 1), 0))

    return pl.pallas_call(
        functools.partial(_ffn_kernel, alpha=alpha, d_ff=d_ff, n_chunks=n_chunks),
        out_shape=jax.ShapeDtypeStruct((B, S, D), F32),
        grid=(B, nt),
        in_specs=[
            main(D), prev(D, HALO, r8), nxt(D, HALO, r8),
            main(wa), prev(wa, BF16_ROWS, r16), nxt(wa, BF16_ROWS, r16),
            main(wb), prev(wb, BF16_ROWS, r16), nxt(wb, BF16_ROWS, r16),
            const(*lne_g.shape), const(*lne_b.shape), const(*on_g.shape), const(*w_o.shape),
            const(*ln1_g.shape), const(*ln1_b.shape), const(*w_up.shape), const(*conv_w.shape),
            const(*conv_b.shape), const(*w_down.shape), const(*ln2_g.shape), const(*ln2_b.shape),
        ],
        out_specs=pl.BlockSpec((1, tm, D), lambda b, i: (b, i, 0)),
        scratch_shapes=[pltpu.VMEM((D // LANES, tm + 2 * HALO, LANES), F32),
                        pltpu.VMEM((tm + 4 * HALO, d_ff // n_chunks), F32)],
        compiler_params=pltpu.CompilerParams(
            dimension_semantics=("parallel", "parallel"), vmem_limit_bytes=VMEM_LIMIT),
        name="ffn",
    )(x, x, x, o_a, o_a, o_a, o_b, o_b, o_b, lne_g, lne_b, on_g, w_o, ln1_g, ln1_b, w_up, conv_w, conv_b,
      w_down, ln2_g, ln2_b)


def _layer(x, positions, ln_emb_g, ln_emb_b, w_in, q_norm_g, w_uq, kv_norm_g, w_ukv, out_norm_g, w_o,
           ln1_g, ln1_b, w_up, conv_w, conv_b, w_down, ln2_g, ln2_b, *, alpha, tm_proj, tq, tm_ffn, n_chunks):
    B, S, D = x.shape
    row = lambda v: v.reshape(1, -1).astype(F32)
    w_in_p = jnp.concatenate([w_in, jnp.zeros((D, LANES - MLA_ROPE_DIM), w_in.dtype)], axis=1).astype(BF16)
    q_rank = w_uq.shape[0]
    wq = w_uq.reshape(q_rank, MLA_HEADS, MLA_QK_DIM)
    wq = jnp.pad(wq, ((0, 0), (0, 0), (0, MLA_QK_PAD - MLA_QK_DIM)))
    wuqt = wq.reshape(q_rank, MLA_HEADS * MLA_QK_PAD).T.astype(BF16)
    kv_rank = w_ukv.shape[0]
    wkv = w_ukv.reshape(kv_rank, MLA_HEADS, MLA_NOPE_DIM + MLA_V_DIM)
    wuk = wkv[:, :, :MLA_NOPE_DIM].reshape(kv_rank, MLA_HEADS * MLA_NOPE_DIM).astype(BF16)
    wuvt = wkv[:, :, MLA_NOPE_DIM:].reshape(kv_rank, MLA_HEADS * MLA_V_DIM).T.astype(BF16)

    def inv_freq(rot_dim):
        half = rot_dim // 2
        return jnp.power(jnp.float32(ROPE_THETA),
                         -jnp.arange(half, dtype=F32) * (2.0 / rot_dim)).reshape(half, 1)

    pos_row = positions.astype(F32).reshape(B, 1, S)
    q_a, k_a, v_a, qt, kf, vt = _proj_call(
        x, pos_row, inv_freq(SWA_ROT_DIM), inv_freq(MLA_ROPE_DIM), row(ln_emb_g), row(ln_emb_b), w_in_p,
        row(q_norm_g), row(kv_norm_g), wuqt, wuk, wuvt, tm=tm_proj)
    o_a = _swa_call(q_a, k_a, v_a)
    o_b = _mla_call(qt, kf, vt, tq=tq)
    return _ffn_call(x, o_a, o_b, row(ln_emb_g), row(ln_emb_b), row(out_norm_g), w_o.astype(BF16),
                     row(ln1_g), row(ln1_b), w_up.astype(BF16), conv_w.astype(F32), row(conv_b),
                     w_down.astype(BF16), row(ln2_g), row(ln2_b), tm=tm_ffn, alpha=alpha, n_chunks=n_chunks)


def kernel(x, positions, ln_emb_g, ln_emb_b, w_in, q_norm_g, w_uq, kv_norm_g, w_ukv, out_norm_g, w_o,
           ln1_g, ln1_b, w_up, conv_w, conv_b, w_down, ln2_g, ln2_b):
    depth = w_in.shape[0]
    assert depth == 1, "single-layer stack"
    alpha = (2.0 * depth) ** 0.25
    S = x.shape[1]
    tile = min(512, S)
    return _layer(x, positions, ln_emb_g, ln_emb_b, w_in[0], q_norm_g[0], w_uq[0], kv_norm_g[0], w_ukv[0],
                  out_norm_g[0], w_o[0], ln1_g[0], ln1_b[0], w_up[0], conv_w[0], conv_b[0], w_down[0],
                  ln2_g[0], ln2_b[0], alpha=alpha, tm_proj=tile, tq=tile, tm_ffn=tile, n_chunks=2)
```

```python
import functools
import math

import jax
import jax.numpy as jnp
from jax import lax
from jax.experimental import pallas as pl
from jax.experimental.pallas import tpu as pltpu

F32 = jnp.float32
BF16 = jnp.bfloat16

SWA_HEAD_DIM = 64
SWA_PATTERNS = ((128, 1), (512, 4), (2048, 16))
SWA_ROT_DIM = 16
MLA_NOPE_DIM = 128
MLA_ROPE_DIM = 64
MLA_V_DIM = 128
MLA_QK_DIM = MLA_NOPE_DIM + MLA_ROPE_DIM
MLA_HEADS = 4
MLA_QK_PAD = 256
ROPE_THETA = 500000.0
LN_EPS = 1e-5
RMS_EPS = 1e-6
NEG_INF = -1e30
CONV_WIDTH = 3
LOG2E = math.log2(math.e)

LANES = 128
HALO = 8
BF16_ROWS = 16
MLA_VT_ROWS = MLA_V_DIM + BF16_ROWS
MLA_JUMP_LIMIT = 60.0
VMEM_LIMIT = 56 * 1024 * 1024


def _dot(a, b):
    return jnp.dot(a, b, preferred_element_type=F32)


def _dot_nt(a, b):
    return lax.dot_general(a, b, (((1,), (1,)), ((), ())), preferred_element_type=F32)


def _layer_norm(x, g, b):
    mu = jnp.mean(x, axis=-1, keepdims=True)
    xc = x - mu
    var = jnp.mean(xc * xc, axis=-1, keepdims=True)
    return xc * lax.rsqrt(var + LN_EPS) * g + b


def _rms_norm(x, g):
    return x * lax.rsqrt(jnp.mean(x * x, axis=-1, keepdims=True) + RMS_EPS) * g


def _proj_kernel(x_ref, pos_ref, invfa_ref, invfm_ref, lng_ref, lnb_ref, win_ref, qng_ref, kvng_ref,
                 wuqt_ref, wuk_ref, wuvt_ref,
                 qa_ref, ka_ref, va_ref, qt_ref, kf_ref, vt_ref, *, swa_w, q_rank, kv_rank):
    tm = x_ref.shape[1]
    xn = _layer_norm(x_ref[0], lng_ref[...], lnb_ref[...]).astype(BF16)
    o_cq = 3 * swa_w
    o_ckv = o_cq + q_rank
    o_kr = o_ckv + kv_rank

    def in_proj(c0, c1):
        return _dot(xn, win_ref[:, c0:c1])

    h_b = in_proj(o_cq, o_kr + LANES)
    h_q = in_proj(0, swa_w)

    pos = pos_ref[0]
    cqn = _rms_norm(h_b[:, 0:q_rank], qng_ref[...]).astype(BF16)
    ckvn = _rms_norm(h_b[:, q_rank:q_rank + kv_rank], kvng_ref[...]).astype(BF16)
    ang_m = invfm_ref[...] * pos
    cos_m, sin_m = jnp.cos(ang_m), jnp.sin(ang_m)
    hm = MLA_ROPE_DIM // 2

    def rope_t(x1, x2):
        return x1 * cos_m - x2 * sin_m, x2 * cos_m + x1 * sin_m

    qt = _dot_nt(wuqt_ref[...], cqn) * (MLA_QK_DIM ** -0.5 * LOG2E)
    for hd in range(MLA_HEADS):
        r0 = hd * MLA_QK_PAD
        qt_ref[0, hd, 0:MLA_NOPE_DIM, :] = qt[r0:r0 + MLA_NOPE_DIM].astype(BF16)
        p0 = r0 + MLA_NOPE_DIM
        n1, n2 = rope_t(qt[p0:p0 + hm], qt[p0 + hm:p0 + 2 * hm])
        qt_ref[0, hd, MLA_NOPE_DIM:MLA_NOPE_DIM + hm, :] = n1.astype(BF16)
        qt_ref[0, hd, MLA_NOPE_DIM + hm:MLA_QK_DIM, :] = n2.astype(BF16)
        qt_ref[0, hd, MLA_QK_DIM:MLA_QK_PAD, :] = jnp.zeros((MLA_QK_PAD - MLA_QK_DIM, tm), BF16)

    kr_t = h_b[:, q_rank + kv_rank:].T
    k1, k2 = rope_t(kr_t[0:hm], kr_t[hm:2 * hm])
    kpe = jnp.concatenate([k1, k2, kr_t[2 * hm:]], axis=0).T.astype(BF16)
    kn = _dot(ckvn, wuk_ref[...])
    for hd in range(MLA_HEADS):
        kf_ref[0, hd, :, 0:MLA_NOPE_DIM] = kn[:, hd * MLA_NOPE_DIM:(hd + 1) * MLA_NOPE_DIM].astype(BF16)
        kf_ref[0, hd, :, MLA_NOPE_DIM:MLA_QK_PAD] = kpe
    vt = _dot_nt(wuvt_ref[...], ckvn)
    for hd in range(MLA_HEADS):
        vt_ref[0, 0, hd, 0:MLA_V_DIM, :] = vt[hd * MLA_V_DIM:(hd + 1) * MLA_V_DIM].astype(BF16)
        vt_ref[0, 0, hd, MLA_V_DIM:MLA_VT_ROWS, :] = jnp.ones((MLA_VT_ROWS - MLA_V_DIM, tm), BF16)

    h_k = in_proj(swa_w, 2 * swa_w)
    ang_a = invfa_ref[...] * pos
    cos_a, sin_a = jnp.cos(ang_a), jnp.sin(ang_a)
    half = SWA_ROT_DIM // 2
    rest = SWA_HEAD_DIM - SWA_ROT_DIM
    ones = jnp.ones((rest, tm), F32)
    zeros = jnp.zeros((rest, tm), F32)
    zhalf = jnp.zeros((half, tm), F32)
    reps = LANES // SWA_HEAD_DIM
    cos_t = jnp.concatenate([cos_a, cos_a, ones] * reps, axis=0).T
    sin_lo = jnp.concatenate([-sin_a, zhalf, zeros] * reps, axis=0).T
    sin_hi = jnp.concatenate([zhalf, sin_a, zeros] * reps, axis=0).T

    def rope_a(xs):
        return (xs * cos_t + pltpu.roll(xs, LANES - half, 1) * sin_lo + pltpu.roll(xs, half, 1) * sin_hi)

    qscale = SWA_HEAD_DIM ** -0.5 * LOG2E
    for j in range(swa_w // LANES):
        c0 = j * LANES
        qa_ref[0, :, c0:c0 + LANES] = (rope_a(h_q[:, c0:c0 + LANES]) * qscale).astype(BF16)
    h_v = in_proj(2 * swa_w, 3 * swa_w)
    for j in range(swa_w // LANES):
        c0 = j * LANES
        ka_ref[0, :, c0:c0 + LANES] = rope_a(h_k[:, c0:c0 + LANES]).astype(BF16)
    va_ref[0] = h_v.astype(BF16)


def _proj_call(x, pos_row, invf_a, invf_m, ln_g, ln_b, w_in_p, qn_g, kvn_g, wuqt, wuk, wuvt, *, tm):
    B, S, D = x.shape
    swa_w = 512
    q_rank = qn_g.shape[-1]
    kv_rank = kvn_g.shape[-1]
    nt = S // tm
    const = lambda *shape: pl.BlockSpec(shape, lambda b, i: (0,) * len(shape))
    out_shape = (
        jax.ShapeDtypeStruct((B, S, swa_w), BF16),
        jax.ShapeDtypeStruct((B, S, swa_w), BF16),
        jax.ShapeDtypeStruct((B, S, swa_w), BF16),
        jax.ShapeDtypeStruct((B, MLA_HEADS, MLA_QK_PAD, S), BF16),
        jax.ShapeDtypeStruct((B, MLA_HEADS, S, MLA_QK_PAD), BF16),
        jax.ShapeDtypeStruct((B, nt, MLA_HEADS, MLA_VT_ROWS, tm), BF16),
    )
    return pl.pallas_call(
        functools.partial(_proj_kernel, swa_w=swa_w, q_rank=q_rank, kv_rank=kv_rank),
        out_shape=out_shape,
        grid=(B, nt),
        in_specs=[
            pl.BlockSpec((1, tm, D), lambda b, i: (b, i, 0)),
            pl.BlockSpec((1, 1, tm), lambda b, i: (b, 0, i)),
            const(*invf_a.shape), const(*invf_m.shape), const(*ln_g.shape), const(*ln_b.shape),
            const(*w_in_p.shape), const(*qn_g.shape), const(*kvn_g.shape),
            const(*wuqt.shape), const(*wuk.shape), const(*wuvt.shape),
        ],
        out_specs=(
            pl.BlockSpec((1, tm, swa_w), lambda b, i: (b, i, 0)),
            pl.BlockSpec((1, tm, swa_w), lambda b, i: (b, i, 0)),
            pl.BlockSpec((1, tm, swa_w), lambda b, i: (b, i, 0)),
            pl.BlockSpec((1, MLA_HEADS, MLA_QK_PAD, tm), lambda b, i: (b, 0, 0, i)),
            pl.BlockSpec((1, MLA_HEADS, tm, MLA_QK_PAD), lambda b, i: (b, 0, i, 0)),
            pl.BlockSpec((1, 1, MLA_HEADS, MLA_VT_ROWS, tm), lambda b, i: (b, i, 0, 0, 0)),
        ),
        compiler_params=pltpu.CompilerParams(
            dimension_semantics=("parallel", "parallel"), vmem_limit_bytes=VMEM_LIMIT),
        name="proj",
    )(x, pos_row, invf_a, invf_m, ln_g, ln_b, w_in_p, qn_g, kvn_g, wuqt, wuk, wuvt)


def _swa_geometry(S, window, d, bq_max):
    L = S // d
    n_side = window // (2 * d)
    bq = min(bq_max, L)
    win = min(L, bq + 2 * n_side)
    return L, n_side, bq, win, L // bq


def _swa_kernel(q_ref, k_ref, v_ref, o_ref, stage, qd, kd, vd, bias, op, lp, ot, lt, *, S, bq_max, unroll):
    lane = lax.broadcasted_iota(jnp.int32, (1, LANES), 1)
    head0 = lane < SWA_HEAD_DIM

    dils = [d for _, d in SWA_PATTERNS if d != 1]
    for src, dst in ((q_ref, qd), (k_ref, kd), (v_ref, vd)):
        stage[0] = src[0].astype(F32)
        cur, d_prev = 0, 1
        for pi, d in enumerate(dils):
            f, L, Lp = d // d_prev, S // d, S // d_prev
            for rp in range(d_prev):
                for q in range(f):
                    r = q * d_prev + rp
                    t = stage[cur, pl.ds(rp * Lp + q, L, stride=f), :]
                    dst[pi, r * L:(r + 1) * L, :] = t.astype(BF16)
                    if pi + 1 < len(dils):
                        stage[1 - cur, r * L:(r + 1) * L, :] = t
            cur, d_prev = 1 - cur, d

    for p, (window, d) in enumerate(SWA_PATTERNS):
        L, n_side, bq, win, nb = _swa_geometry(S, window, d, bq_max)
        rel = (lax.broadcasted_iota(jnp.int32, (bq, win), 0) - lax.broadcasted_iota(jnp.int32, (bq, win), 1))
        ws_last = min(max((nb - 1) * bq - n_side, 0), L - win)
        for c, delta in enumerate((0, n_side, (nb - 1) * bq - ws_last)):
            bias[p, c, 0:bq, 0:win] = jnp.where(jnp.abs(rel + delta) <= n_side, 0.0, NEG_INF)

    pi = 0
    for p, (window, d) in enumerate(SWA_PATTERNS):
        L, n_side, bq, win, nb = _swa_geometry(S, window, d, bq_max)
        if d == 1:
            load_q = lambda r0, n: q_ref[0, pl.ds(r0, n), :]
            load_k = lambda r0, n: k_ref[0, pl.ds(r0, n), :]
            load_v = lambda r0, n: v_ref[0, pl.ds(r0, n), :]
        else:
            load_q = functools.partial(lambda r0, n, j: qd[j, pl.ds(r0, n), :], j=pi)
            load_k = functools.partial(lambda r0, n, j: kd[j, pl.ds(r0, n), :], j=pi)
            load_v = functools.partial(lambda r0, n, j: vd[j, pl.ds(r0, n), :], j=pi)
            pi += 1

        def body(n, carry, L=L, n_side=n_side, bq=bq, win=win, nb=nb, p=p,
                 load_q=load_q, load_k=load_k, load_v=load_v):
            r = n // nb
            i = n % nb
            row0 = pl.multiple_of(r * L + i * bq, bq)
            ws = jnp.clip(i * bq - n_side, 0, L - win)
            krow0 = pl.multiple_of(r * L + ws, BF16_ROWS)
            q = load_q(row0, bq)
            k = load_k(krow0, win)
            v1 = jnp.concatenate([load_v(krow0, win), jnp.ones((win, LANES), BF16)], axis=1)
            mask = bias[p, jnp.where(i == 0, 0, jnp.where(i == nb - 1, 2, 1)), 0:bq, 0:win]
            outs, lses = [], []
            for hsel in (head0, jnp.logical_not(head0)):
                qh = jnp.where(hsel, q, jnp.zeros_like(q))
                s = _dot_nt(qh, k) + mask
                m = jnp.max(s, axis=1, keepdims=True)
                e = jnp.exp2(s - m).astype(BF16)
                oa = _dot(e, v1)
                den = oa[:, LANES:]
                outs.append(oa[:, :LANES] / den)
                lses.append(m + jnp.log2(den))
            op[p, pl.ds(row0, bq), :] = jnp.where(head0, outs[0], outs[1])
            lp[p, pl.ds(row0, bq), :] = jnp.where(head0, lses[0], lses[1])
            return carry

        lax.fori_loop(0, S // bq, body, 0, unroll=unroll)

    def reader(ref, lead):
        return lambda rows: ref[lead, rows, :]

    def writer(ref, lead):
        def write(rows, val):
            ref[lead, rows, :] = val
        return write

    def spread(read, write, d, d_prev):
        f, L, Lp = d // d_prev, S // d, S // d_prev
        for rp in range(d_prev):
            for q in range(f):
                r = q * d_prev + rp
                write(pl.ds(rp * Lp + q, L, stride=f), read(slice(r * L, (r + 1) * L)))

    pi = 0
    for p, (_, d) in enumerate(SWA_PATTERNS):
        if d == 1:
            continue
        chain = [1] + dils[:pi + 1]
        for src, out in ((op, ot), (lp, lt)):
            read = reader(src, p)
            for step in range(pi, -1, -1):
                write = writer(out, pi) if step == 0 else writer(stage, step % 2)
                spread(read, write, chain[step + 1], chain[step])
                read = reader(stage, step % 2)
        pi += 1

    cr = min(512, S)

    def combine(c, carry):
        r0 = pl.multiple_of(c * cr, cr)
        os_, ls_ = [], []
        pj = 0
        for p, (_, d) in enumerate(SWA_PATTERNS):
            if d == 1:
                os_.append(op[p, pl.ds(r0, cr), :])
                ls_.append(lp[p, pl.ds(r0, cr), :])
            else:
                os_.append(ot[pj, pl.ds(r0, cr), :])
                ls_.append(lt[pj, pl.ds(r0, cr), :])
                pj += 1
        m = functools.reduce(jnp.maximum, ls_)
        es = [jnp.exp2(l - m) for l in ls_]
        den = functools.reduce(lambda a, b: a + b, es)
        num = functools.reduce(lambda a, b: a + b, [e * o for e, o in zip(es, os_)])
        o_ref[0, pl.ds(r0, cr), :] = (num / den).astype(o_ref.dtype)
        return carry

    lax.fori_loop(0, S // cr, combine, 0)


def _swa_call(q_a, k_a, v_a):
    B, S, W = q_a.shape
    npat = len(SWA_PATTERNS)
    nd = sum(1 for _, d in SWA_PATTERNS if d != 1)
    bq_max = 128
    geo = [_swa_geometry(S, w, d, bq_max) for w, d in SWA_PATTERNS]
    bq, win = max(g[2] for g in geo), max(g[3] for g in geo)
    spec = pl.BlockSpec((1, S, LANES), lambda b, j: (b, 0, j))
    return pl.pallas_call(
        functools.partial(_swa_kernel, S=S, bq_max=bq_max, unroll=8),
        out_shape=jax.ShapeDtypeStruct((B, S, W), BF16),
        grid=(B, W // LANES),
        in_specs=[spec, spec, spec],
        out_specs=spec,
        scratch_shapes=[
            pltpu.VMEM((2, S, LANES), F32),
            pltpu.VMEM((nd, S, LANES), BF16),
            pltpu.VMEM((nd, S, LANES), BF16),
            pltpu.VMEM((nd, S, LANES), BF16),
            pltpu.VMEM((npat, 3, bq, win), F32),
            pltpu.VMEM((npat, S, LANES), F32),
            pltpu.VMEM((npat, S, LANES), F32),
            pltpu.VMEM((nd, S, LANES), F32),
            pltpu.VMEM((nd, S, LANES), F32),
        ],
        compiler_params=pltpu.CompilerParams(
            dimension_semantics=("parallel", "parallel"), vmem_limit_bytes=VMEM_LIMIT),
        name="swa",
    )(q_a, k_a, v_a)


def _mla_kernel(qt_ref, k_ref, vt_ref, o_ref, acc_ref):
    qt = qt_ref[0, 0]
    tq = qt.shape[1]
    nk, tk = vt_ref.shape[1], vt_ref.shape[4]

    def scores(j):
        return _dot(k_ref[0, 0, j * tk:(j + 1) * tk, :], qt)

    def pv(j, e):
        return _dot(vt_ref[0, j, 0], e.astype(BF16))

    def finish():
        acc = acc_ref[...]
        o_ref[0] = (acc[:MLA_V_DIM] / acc[MLA_V_DIM:MLA_V_DIM + 1]).T.astype(o_ref.dtype)

    st = scores(0)
    st_next = scores(1) if nk > 1 else None
    m = jnp.max(st, axis=0, keepdims=True)
    acc_ref[...] = pv(0, jnp.exp2(st - m))
    jump = jnp.zeros((1, tq), F32)
    for j in range(1, nk):
        st, st_next = st_next, (scores(j + 1) if j + 1 < nk else None)
        mt = jnp.max(st, axis=0, keepdims=True)
        acc = acc_ref[...] + pv(j, jnp.exp2(st - m))
        jump = jnp.maximum(jump, mt - m)
        m_new = jnp.maximum(m, mt)
        acc_ref[...] = acc * jnp.exp2(m - m_new)
        m = m_new
    in_range = jnp.max(jump) <= MLA_JUMP_LIMIT

    @pl.when(in_range)
    def _():
        finish()

    @pl.when(jnp.logical_not(in_range))
    def _():
        acc_ref[...] = jnp.zeros_like(acc_ref)
        st = scores(0)
        m = jnp.full((1, tq), NEG_INF, F32)
        for j in range(nk):
            st_next = scores(j + 1) if j + 1 < nk else None
            m_new = jnp.maximum(m, jnp.max(st, axis=0, keepdims=True))
            acc_ref[...] = jnp.exp2(m - m_new) * acc_ref[...] + pv(j, jnp.exp2(st - m_new))
            st, m = st_next, m_new
        finish()


def _mla_call(qt, kf, vt, *, tq):
    B, H, QP, S = qt.shape
    nk, tk = vt.shape[1], vt.shape[4]
    return pl.pallas_call(
        _mla_kernel,
        out_shape=jax.ShapeDtypeStruct((B, S, H * MLA_V_DIM), BF16),
        grid=(B, H, S // tq),
        in_specs=[
            pl.BlockSpec((1, 1, QP, tq), lambda b, h, i: (b, h, 0, i)),
            pl.BlockSpec((1, 1, S, QP), lambda b, h, i: (b, h, 0, 0)),
            pl.BlockSpec((1, nk, 1, MLA_VT_ROWS, tk), lambda b, h, i: (b, 0, h, 0, 0)),
        ],
        out_specs=pl.BlockSpec((1, tq, MLA_V_DIM), lambda b, h, i: (b, i, h)),
        scratch_shapes=[pltpu.VMEM((MLA_VT_ROWS, tq), F32)],
        compiler_params=pltpu.CompilerParams(
            dimension_semantics=("parallel", "parallel", "parallel"), vmem_limit_bytes=VMEM_LIMIT),
        name="mla",
    )(qt, kf, vt)


def _ffn_kernel(x_ref, xp_ref, xn_ref, oa_ref, oap_ref, oan_ref, ob_ref, obp_ref, obn_ref,
                lneg_ref, lneb_ref, ong_ref, wo_ref, ln1g_ref, ln1b_ref, wup_ref, cw_ref, cb_ref,
                wdn_ref, ln2g_ref, ln2b_ref, out_ref, st, *, alpha, d_ff, n_chunks):
    tm = x_ref.shape[1]
    rows = tm + 2 * HALO
    G = rows // HALO
    i = pl.program_id(1)
    last = pl.num_programs(1) - 1
    hi = BF16_ROWS - HALO
    nslab = x_ref.shape[2] // LANES

    def with_halo(main, prev, nxt):
        return jnp.concatenate([prev, main, nxt], axis=0)

    x = with_halo(x_ref[0], xp_ref[0], xn_ref[0])
    oa = with_halo(oa_ref[0].astype(F32), oap_ref[0].astype(F32)[hi:], oan_ref[0].astype(F32)[:HALO])
    ob = with_halo(ob_ref[0].astype(F32), obp_ref[0].astype(F32)[hi:], obn_ref[0].astype(F32)[:HALO])
    wa = oa.shape[1]
    xn = _layer_norm(x, lneg_ref[...], lneb_ref[...])
    o = jnp.concatenate([_rms_norm(oa, ong_ref[:, :wa]), _rms_norm(ob, ong_ref[:, wa:])], axis=1)
    x1 = _layer_norm(alpha * xn + _dot(o.astype(BF16), wo_ref[...]), ln1g_ref[...], ln1b_ref[...])

    ridx = lax.broadcasted_iota(jnp.int32, (rows, 1), 0)
    inside = jnp.logical_and(jnp.logical_or(ridx >= HALO, i > 0),
                             jnp.logical_or(ridx < tm + HALO, i < last))
    x1m = jnp.where(inside, x1, 0.0)

    for j in range(nslab):
        st[j] = x1m[:, j * LANES:(j + 1) * LANES]
    x1p = jnp.concatenate(
        [jnp.concatenate([st[j, pl.ds(g, HALO, stride=G), :] for j in range(nslab)], axis=1) for g in range(G)],
        axis=0).astype(BF16)

    ck = d_ff // n_chunks

    def up(base):
        return _dot(x1p, wup_ref[:, base:base + ck])

    def conv(u, base):
        before = jnp.concatenate([pltpu.roll(u[rows - HALO:rows], 1, 0), u[0:rows - HALO]], axis=0)
        after = jnp.concatenate([u[HALO:rows], pltpu.roll(u[0:HALO], HALO - 1, 0)], axis=0)
        cw = cw_ref[:, base:base + ck]
        return cb_ref[:, base:base + ck] + before * cw[0:1] + u * cw[1:2] + after * cw[2:3]

    bases = [(c * ck, d_ff + c * ck) for c in range(n_chunks)]
    u_gate, u_val = up(bases[0][0]), up(bases[0][1])
    y = None
    for c in range(n_chunks):
        more = c + 1 < n_chunks
        gate = conv(u_gate, bases[c][0])
        u_gate = up(bases[c + 1][0]) if more else None
        val = conv(u_val, bases[c][1])
        g = (gate / (1.0 + jnp.exp2(gate * (-LOG2E))) * val).astype(BF16)
        u_val = up(bases[c + 1][1]) if more else None
        yc = _dot(g, wdn_ref[c * ck:(c + 1) * ck, :])
        y = yc if y is None else y + yc
    for g in range(G):
        for j in range(nslab):
            st[j, pl.ds(g, HALO, stride=G), :] = y[g * HALO:(g + 1) * HALO, j * LANES:(j + 1) * LANES]
    y = jnp.concatenate([st[j, HALO:HALO + tm, :] for j in range(nslab)], axis=1)
    out_ref[0] = _layer_norm(alpha * x1[HALO:HALO + tm] + y, ln2g_ref[...], ln2b_ref[...])


def _ffn_call(x, o_a, o_b, lne_g, lne_b, on_g, w_o, ln1_g, ln1_b, w_up, conv_w, conv_b, w_down,
              ln2_g, ln2_b, *, tm, alpha, n_chunks):
    B, S, D = x.shape
    d_ff = w_down.shape[0]
    wa, wb = o_a.shape[-1], o_b.shape[-1]
    nt = S // tm
    r8, r16 = tm // HALO, tm // BF16_ROWS
    const = lambda *shape: pl.BlockSpec(shape, lambda b, i: (0,) * len(shape), pipeline_mode=pl.Buffered(1))

    def main(w):
        return pl.BlockSpec((1, tm, w), lambda b, i: (b, i, 0))

    def prev(w, rb, r):
        return pl.BlockSpec((1, rb, w), lambda b, i: (b, jnp.maximum(i * r - 1, 0), 0))

    def nxt(w, rb, r):
        return pl.BlockSpec((1, rb, w), lambda b, i: (b, jnp.minimum((i + 1) * r, S // rb - 1), 0))

    return pl.pallas_call(
        functools.partial(_ffn_kernel, alpha=alpha, d_ff=d_ff, n_chunks=n_chunks),
        out_shape=jax.ShapeDtypeStruct((B, S, D), F32),
        grid=(B, nt),
        in_specs=[
            main(D), prev(D, HALO, r8), nxt(D, HALO, r8),
            main(wa), prev(wa, BF16_ROWS, r16), nxt(wa, BF16_ROWS, r16),
            main(wb), prev(wb, BF16_ROWS, r16), nxt(wb, BF16_ROWS, r16),
            const(*lne_g.shape), const(*lne_b.shape), const(*on_g.shape), const(*w_o.shape),
            const(*ln1_g.shape), const(*ln1_b.shape), const(*w_up.shape), const(*conv_w.shape),
            const(*conv_b.shape), const(*w_down.shape), const(*ln2_g.shape), const(*ln2_b.shape),
        ],
        out_specs=pl.BlockSpec((1, tm, D), lambda b, i: (b, i, 0)),
        scratch_shapes=[pltpu.VMEM((D // LANES, tm + 2 * HALO, LANES), F32)],
        compiler_params=pltpu.CompilerParams(
            dimension_semantics=("parallel", "parallel"), vmem_limit_bytes=VMEM_LIMIT),
        name="ffn",
    )(x, x, x, o_a, o_a, o_a, o_b, o_b, o_b, lne_g, lne_b, on_g, w_o, ln1_g, ln1_b, w_up, conv_w, conv_b,
      w_down, ln2_g, ln2_b)


def _layer(x, positions, ln_emb_g, ln_emb_b, w_in, q_norm_g, w_uq, kv_norm_g, w_ukv, out_norm_g, w_o,
           ln1_g, ln1_b, w_up, conv_w, conv_b, w_down, ln2_g, ln2_b, *, alpha, tm_proj, tq, tm_ffn, n_chunks):
    B, S, D = x.shape
    row = lambda v: v.reshape(1, -1).astype(F32)
    w_in_p = jnp.concatenate([w_in, jnp.zeros((D, LANES - MLA_ROPE_DIM), w_in.dtype)], axis=1).astype(BF16)
    q_rank = w_uq.shape[0]
    wq = w_uq.reshape(q_rank, MLA_HEADS, MLA_QK_DIM)
    wq = jnp.pad(wq, ((0, 0), (0, 0), (0, MLA_QK_PAD - MLA_QK_DIM)))
    wuqt = wq.reshape(q_rank, MLA_HEADS * MLA_QK_PAD).T.astype(BF16)
    kv_rank = w_ukv.shape[0]
    wkv = w_ukv.reshape(kv_rank, MLA_HEADS, MLA_NOPE_DIM + MLA_V_DIM)
    wuk = wkv[:, :, :MLA_NOPE_DIM].reshape(kv_rank, MLA_HEADS * MLA_NOPE_DIM).astype(BF16)
    wuvt = wkv[:, :, MLA_NOPE_DIM:].reshape(kv_rank, MLA_HEADS * MLA_V_DIM).T.astype(BF16)

    def inv_freq(rot_dim):
        half = rot_dim // 2
        return jnp.power(jnp.float32(ROPE_THETA),
                         -jnp.arange(half, dtype=F32) * (2.0 / rot_dim)).reshape(half, 1)

    pos_row = positions.astype(F32).reshape(B, 1, S)
    q_a, k_a, v_a, qt, kf, vt = _proj_call(
        x, pos_row, inv_freq(SWA_ROT_DIM), inv_freq(MLA_ROPE_DIM), row(ln_emb_g), row(ln_emb_b), w_in_p,
        row(q_norm_g), row(kv_norm_g), wuqt, wuk, wuvt, tm=tm_proj)
    o_a = _swa_call(q_a, k_a, v_a)
    o_b = _mla_call(qt, kf, vt, tq=tq)
    return _ffn_call(x, o_a, o_b, row(ln_emb_g), row(ln_emb_b), row(out_norm_g), w_o.astype(BF16),
                     row(ln1_g), row(ln1_b), w_up.astype(BF16), conv_w.astype(F32), row(conv_b),
                     w_down.astype(BF16), row(ln2_g), row(ln2_b), tm=tm_ffn, alpha=alpha, n_chunks=n_chunks)


def kernel(x, positions, ln_emb_g, ln_emb_b, w_in, q_norm_g, w_uq, kv_norm_g, w_ukv, out_norm_g, w_o,
           ln1_g, ln1_b, w_up, conv_w, conv_b, w_down, ln2_g, ln2_b):
    depth = w_in.shape[0]
    assert depth == 1, "single-layer stack"
    alpha = (2.0 * depth) ** 0.25
    S = x.shape[1]
    tile = min(512, S)
    return _layer(x, positions, ln_emb_g, ln_emb_b, w_in[0], q_norm_g[0], w_uq[0], kv_norm_g[0], w_ukv[0],
                  out_norm_g[0], w_o[0], ln1_g[0], ln1_b[0], w_up[0], conv_w[0], conv_b[0], w_down[0],
                  ln2_g[0], ln2_b[0], alpha=alpha, tm_proj=tile, tq=tile, tm_ffn=tile, n_chunks=2)
```

```python
import functools
import math

import jax
import jax.numpy as jnp
from jax import lax
from jax.experimental import pallas as pl
from jax.experimental.pallas import tpu as pltpu

F32 = jnp.float32
BF16 = jnp.bfloat16

SWA_HEAD_DIM = 64
SWA_PATTERNS = ((128, 1), (512, 4), (2048, 16))
SWA_ROT_DIM = 16
MLA_NOPE_DIM = 128
MLA_ROPE_DIM = 64
MLA_V_DIM = 128
MLA_QK_DIM = MLA_NOPE_DIM + MLA_ROPE_DIM
MLA_HEADS = 4
MLA_QK_PAD = 256
ROPE_THETA = 500000.0
LN_EPS = 1e-5
RMS_EPS = 1e-6
NEG_INF = -1e30
CONV_WIDTH = 3
LOG2E = math.log2(math.e)

LANES = 128
HALO = 8
BF16_ROWS = 16
MLA_VT_ROWS = MLA_V_DIM + BF16_ROWS
MLA_JUMP_LIMIT = 60.0
MLA_FIRST_ROWS = 128
VMEM_LIMIT = 56 * 1024 * 1024


def _dot(a, b):
    return jnp.dot(a, b, preferred_element_type=F32)


def _dot_nt(a, b):
    return lax.dot_general(a, b, (((1,), (1,)), ((), ())), preferred_element_type=F32)


def _layer_norm(x, g, b):
    mu = jnp.mean(x, axis=-1, keepdims=True)
    xc = x - mu
    var = jnp.mean(xc * xc, axis=-1, keepdims=True)
    return xc * lax.rsqrt(var + LN_EPS) * g + b


def _rms_norm(x, g):
    return x * lax.rsqrt(jnp.mean(x * x, axis=-1, keepdims=True) + RMS_EPS) * g


def _proj_kernel(x_ref, pos_ref, invfa_ref, invfm_ref, lng_ref, lnb_ref, win_ref, qng_ref, kvng_ref,
                 wuqt_ref, wuk_ref, wuvt_ref,
                 qa_ref, ka_ref, va_ref, qt_ref, kf_ref, vt_ref, *, swa_w, q_rank, kv_rank):
    tm = x_ref.shape[1]
    xn = _layer_norm(x_ref[0], lng_ref[...], lnb_ref[...]).astype(BF16)
    o_cq = 3 * swa_w
    o_ckv = o_cq + q_rank
    o_kr = o_ckv + kv_rank

    def in_proj(c0, c1):
        return _dot(xn, win_ref[:, c0:c1])

    h_b = in_proj(o_cq, o_kr + LANES)
    h_q = in_proj(0, swa_w)

    pos = pos_ref[0]
    cqn = _rms_norm(h_b[:, 0:q_rank], qng_ref[...]).astype(BF16)
    ckvn = _rms_norm(h_b[:, q_rank:q_rank + kv_rank], kvng_ref[...]).astype(BF16)
    ang_m = invfm_ref[...] * pos
    cos_m, sin_m = jnp.cos(ang_m), jnp.sin(ang_m)
    hm = MLA_ROPE_DIM // 2

    def rope_t(x1, x2):
        return x1 * cos_m - x2 * sin_m, x2 * cos_m + x1 * sin_m

    qt = _dot_nt(wuqt_ref[...], cqn) * (MLA_QK_DIM ** -0.5 * LOG2E)
    for hd in range(MLA_HEADS):
        r0 = hd * MLA_QK_PAD
        qt_ref[0, hd, 0:MLA_NOPE_DIM, :] = qt[r0:r0 + MLA_NOPE_DIM].astype(BF16)
        p0 = r0 + MLA_NOPE_DIM
        n1, n2 = rope_t(qt[p0:p0 + hm], qt[p0 + hm:p0 + 2 * hm])
        qt_ref[0, hd, MLA_NOPE_DIM:MLA_NOPE_DIM + hm, :] = n1.astype(BF16)
        qt_ref[0, hd, MLA_NOPE_DIM + hm:MLA_QK_DIM, :] = n2.astype(BF16)
        qt_ref[0, hd, MLA_QK_DIM:MLA_QK_PAD, :] = jnp.zeros((MLA_QK_PAD - MLA_QK_DIM, tm), BF16)

    kr_t = h_b[:, q_rank + kv_rank:].T
    k1, k2 = rope_t(kr_t[0:hm], kr_t[hm:2 * hm])
    kpe = jnp.concatenate([k1, k2, kr_t[2 * hm:]], axis=0).T.astype(BF16)
    kn = _dot(ckvn, wuk_ref[...])
    for hd in range(MLA_HEADS):
        kf_ref[0, hd, :, 0:MLA_NOPE_DIM] = kn[:, hd * MLA_NOPE_DIM:(hd + 1) * MLA_NOPE_DIM].astype(BF16)
        kf_ref[0, hd, :, MLA_NOPE_DIM:MLA_QK_PAD] = kpe
    vt = _dot_nt(wuvt_ref[...], ckvn)
    for hd in range(MLA_HEADS):
        vt_ref[0, 0, hd, 0:MLA_V_DIM, :] = vt[hd * MLA_V_DIM:(hd + 1) * MLA_V_DIM].astype(BF16)
        vt_ref[0, 0, hd, MLA_V_DIM:MLA_VT_ROWS, :] = jnp.ones((MLA_VT_ROWS - MLA_V_DIM, tm), BF16)

    h_k = in_proj(swa_w, 2 * swa_w)
    ang_a = invfa_ref[...] * pos
    cos_a, sin_a = jnp.cos(ang_a), jnp.sin(ang_a)
    half = SWA_ROT_DIM // 2
    rest = SWA_HEAD_DIM - SWA_ROT_DIM
    ones = jnp.ones((rest, tm), F32)
    zeros = jnp.zeros((rest, tm), F32)
    zhalf = jnp.zeros((half, tm), F32)
    reps = LANES // SWA_HEAD_DIM
    cos_t = jnp.concatenate([cos_a, cos_a, ones] * reps, axis=0).T
    sin_lo = jnp.concatenate([-sin_a, zhalf, zeros] * reps, axis=0).T
    sin_hi = jnp.concatenate([zhalf, sin_a, zeros] * reps, axis=0).T

    def rope_a(xs):
        return (xs * cos_t + pltpu.roll(xs, LANES - half, 1) * sin_lo + pltpu.roll(xs, half, 1) * sin_hi)

    qscale = SWA_HEAD_DIM ** -0.5 * LOG2E
    for j in range(swa_w // LANES):
        c0 = j * LANES
        qa_ref[0, :, c0:c0 + LANES] = (rope_a(h_q[:, c0:c0 + LANES]) * qscale).astype(BF16)
    h_v = in_proj(2 * swa_w, 3 * swa_w)
    for j in range(swa_w // LANES):
        c0 = j * LANES
        ka_ref[0, :, c0:c0 + LANES] = rope_a(h_k[:, c0:c0 + LANES]).astype(BF16)
    va_ref[0] = h_v.astype(BF16)


def _proj_call(x, pos_row, invf_a, invf_m, ln_g, ln_b, w_in_p, qn_g, kvn_g, wuqt, wuk, wuvt, *, tm):
    B, S, D = x.shape
    swa_w = 512
    q_rank = qn_g.shape[-1]
    kv_rank = kvn_g.shape[-1]
    nt = S // tm
    const = lambda *shape: pl.BlockSpec(shape, lambda b, i: (0,) * len(shape))
    out_shape = (
        jax.ShapeDtypeStruct((B, S, swa_w), BF16),
        jax.ShapeDtypeStruct((B, S, swa_w), BF16),
        jax.ShapeDtypeStruct((B, S, swa_w), BF16),
        jax.ShapeDtypeStruct((B, MLA_HEADS, MLA_QK_PAD, S), BF16),
        jax.ShapeDtypeStruct((B, MLA_HEADS, S, MLA_QK_PAD), BF16),
        jax.ShapeDtypeStruct((B, nt, MLA_HEADS, MLA_VT_ROWS, tm), BF16),
    )
    return pl.pallas_call(
        functools.partial(_proj_kernel, swa_w=swa_w, q_rank=q_rank, kv_rank=kv_rank),
        out_shape=out_shape,
        grid=(B, nt),
        in_specs=[
            pl.BlockSpec((1, tm, D), lambda b, i: (b, i, 0)),
            pl.BlockSpec((1, 1, tm), lambda b, i: (b, 0, i)),
            const(*invf_a.shape), const(*invf_m.shape), const(*ln_g.shape), const(*ln_b.shape),
            const(*w_in_p.shape), const(*qn_g.shape), const(*kvn_g.shape),
            const(*wuqt.shape), const(*wuk.shape), const(*wuvt.shape),
        ],
        out_specs=(
            pl.BlockSpec((1, tm, swa_w), lambda b, i: (b, i, 0)),
            pl.BlockSpec((1, tm, swa_w), lambda b, i: (b, i, 0)),
            pl.BlockSpec((1, tm, swa_w), lambda b, i: (b, i, 0)),
            pl.BlockSpec((1, MLA_HEADS, MLA_QK_PAD, tm), lambda b, i: (b, 0, 0, i)),
            pl.BlockSpec((1, MLA_HEADS, tm, MLA_QK_PAD), lambda b, i: (b, 0, i, 0)),
            pl.BlockSpec((1, 1, MLA_HEADS, MLA_VT_ROWS, tm), lambda b, i: (b, i, 0, 0, 0)),
        ),
        compiler_params=pltpu.CompilerParams(
            dimension_semantics=("parallel", "parallel"), vmem_limit_bytes=VMEM_LIMIT),
        name="proj",
    )(x, pos_row, invf_a, invf_m, ln_g, ln_b, w_in_p, qn_g, kvn_g, wuqt, wuk, wuvt)


def _swa_geometry(S, window, d, bq_max):
    L = S // d
    n_side = window // (2 * d)
    bq = min(bq_max, L)
    win = min(L, bq + 2 * n_side)
    return L, n_side, bq, win, L // bq


def _swa_kernel(q_ref, k_ref, v_ref, o_ref, stage, qd, kd, vd, bias, op, lp, ot, lt, *, S, bq_max, unroll):
    lane = lax.broadcasted_iota(jnp.int32, (1, LANES), 1)
    head0 = lane < SWA_HEAD_DIM

    dils = [d for _, d in SWA_PATTERNS if d != 1]
    for src, dst in ((q_ref, qd), (k_ref, kd), (v_ref, vd)):
        stage[0] = src[0].astype(F32)
        cur, d_prev = 0, 1
        for pi, d in enumerate(dils):
            f, L, Lp = d // d_prev, S // d, S // d_prev
            for rp in range(d_prev):
                for q in range(f):
                    r = q * d_prev + rp
                    t = stage[cur, pl.ds(rp * Lp + q, L, stride=f), :]
                    dst[pi, r * L:(r + 1) * L, :] = t.astype(BF16)
                    if pi + 1 < len(dils):
                        stage[1 - cur, r * L:(r + 1) * L, :] = t
            cur, d_prev = 1 - cur, d

    for p, (window, d) in enumerate(SWA_PATTERNS):
        L, n_side, bq, win, nb = _swa_geometry(S, window, d, bq_max)
        rel = (lax.broadcasted_iota(jnp.int32, (bq, win), 0) - lax.broadcasted_iota(jnp.int32, (bq, win), 1))
        ws_last = min(max((nb - 1) * bq - n_side, 0), L - win)
        for c, delta in enumerate((0, n_side, (nb - 1) * bq - ws_last)):
            bias[p, c, 0:bq, 0:win] = jnp.where(jnp.abs(rel + delta) <= n_side, 0.0, NEG_INF)

    pi = 0
    for p, (window, d) in enumerate(SWA_PATTERNS):
        L, n_side, bq, win, nb = _swa_geometry(S, window, d, bq_max)
        if d == 1:
            load_q = lambda r0, n: q_ref[0, pl.ds(r0, n), :]
            load_k = lambda r0, n: k_ref[0, pl.ds(r0, n), :]
            load_v = lambda r0, n: v_ref[0, pl.ds(r0, n), :]
        else:
            load_q = functools.partial(lambda r0, n, j: qd[j, pl.ds(r0, n), :], j=pi)
            load_k = functools.partial(lambda r0, n, j: kd[j, pl.ds(r0, n), :], j=pi)
            load_v = functools.partial(lambda r0, n, j: vd[j, pl.ds(r0, n), :], j=pi)
            pi += 1

        def body(n, carry, L=L, n_side=n_side, bq=bq, win=win, nb=nb, p=p,
                 load_q=load_q, load_k=load_k, load_v=load_v):
            r = n // nb
            i = n % nb
            row0 = pl.multiple_of(r * L + i * bq, bq)
            ws = jnp.clip(i * bq - n_side, 0, L - win)
            krow0 = pl.multiple_of(r * L + ws, BF16_ROWS)
            q = load_q(row0, bq)
            k = load_k(krow0, win)
            v1 = jnp.concatenate([load_v(krow0, win), jnp.ones((win, LANES), BF16)], axis=1)
            mask = bias[p, jnp.where(i == 0, 0, jnp.where(i == nb - 1, 2, 1)), 0:bq, 0:win]
            outs, lses = [], []
            for hsel in (head0, jnp.logical_not(head0)):
                qh = jnp.where(hsel, q, jnp.zeros_like(q))
                s = _dot_nt(qh, k) + mask
                m = jnp.max(s, axis=1, keepdims=True)
                e = jnp.exp2(s - m).astype(BF16)
                oa = _dot(e, v1)
                den = oa[:, LANES:]
                outs.append(oa[:, :LANES] / den)
                lses.append(m + jnp.log2(den))
            op[p, pl.ds(row0, bq), :] = jnp.where(head0, outs[0], outs[1])
            lp[p, pl.ds(row0, bq), :] = jnp.where(head0, lses[0], lses[1])
            return carry

        lax.fori_loop(0, S // bq, body, 0, unroll=unroll)

    def reader(ref, lead):
        return lambda rows: ref[lead, rows, :]

    def writer(ref, lead):
        def write(rows, val):
            ref[lead, rows, :] = val
        return write

    def spread(read, write, d, d_prev):
        f, L, Lp = d // d_prev, S // d, S // d_prev
        for rp in range(d_prev):
            for q in range(f):
                r = q * d_prev + rp
                write(pl.ds(rp * Lp + q, L, stride=f), read(slice(r * L, (r + 1) * L)))

    pi = 0
    for p, (_, d) in enumerate(SWA_PATTERNS):
        if d == 1:
            continue
        chain = [1] + dils[:pi + 1]
        for src, out in ((op, ot), (lp, lt)):
            read = reader(src, p)
            for step in range(pi, -1, -1):
                write = writer(out, pi) if step == 0 else writer(stage, step % 2)
                spread(read, write, chain[step + 1], chain[step])
                read = reader(stage, step % 2)
        pi += 1

    cr = min(512, S)

    def combine(c, carry):
        r0 = pl.multiple_of(c * cr, cr)
        os_, ls_ = [], []
        pj = 0
        for p, (_, d) in enumerate(SWA_PATTERNS):
            if d == 1:
                os_.append(op[p, pl.ds(r0, cr), :])
                ls_.append(lp[p, pl.ds(r0, cr), :])
            else:
                os_.append(ot[pj, pl.ds(r0, cr), :])
                ls_.append(lt[pj, pl.ds(r0, cr), :])
                pj += 1
        m = functools.reduce(jnp.maximum, ls_)
        es = [jnp.exp2(l - m) for l in ls_]
        den = functools.reduce(lambda a, b: a + b, es)
        num = functools.reduce(lambda a, b: a + b, [e * o for e, o in zip(es, os_)])
        o_ref[0, pl.ds(r0, cr), :] = (num / den).astype(o_ref.dtype)
        return carry

    lax.fori_loop(0, S // cr, combine, 0)


def _swa_call(q_a, k_a, v_a):
    B, S, W = q_a.shape
    npat = len(SWA_PATTERNS)
    nd = sum(1 for _, d in SWA_PATTERNS if d != 1)
    bq_max = 128
    geo = [_swa_geometry(S, w, d, bq_max) for w, d in SWA_PATTERNS]
    bq, win = max(g[2] for g in geo), max(g[3] for g in geo)
    spec = pl.BlockSpec((1, S, LANES), lambda b, j: (b, 0, j))
    return pl.pallas_call(
        functools.partial(_swa_kernel, S=S, bq_max=bq_max, unroll=16),
        out_shape=jax.ShapeDtypeStruct((B, S, W), BF16),
        grid=(B, W // LANES),
        in_specs=[spec, spec, spec],
        out_specs=spec,
        scratch_shapes=[
            pltpu.VMEM((2, S, LANES), F32),
            pltpu.VMEM((nd, S, LANES), BF16),
            pltpu.VMEM((nd, S, LANES), BF16),
            pltpu.VMEM((nd, S, LANES), BF16),
            pltpu.VMEM((npat, 3, bq, win), F32),
            pltpu.VMEM((npat, S, LANES), F32),
            pltpu.VMEM((npat, S, LANES), F32),
            pltpu.VMEM((nd, S, LANES), F32),
            pltpu.VMEM((nd, S, LANES), F32),
        ],
        compiler_params=pltpu.CompilerParams(
            dimension_semantics=("parallel", "parallel"), vmem_limit_bytes=VMEM_LIMIT),
        name="swa",
    )(q_a, k_a, v_a)


def _mla_kernel(qt_ref, k_ref, vt_ref, o_ref, acc_ref):
    qt = qt_ref[0, 0]
    tq = qt.shape[1]
    nk, tk = vt_ref.shape[1], vt_ref.shape[4]

    def scores(j):
        return _dot(k_ref[0, 0, j * tk:(j + 1) * tk, :], qt)

    def pv(j, e):
        return _dot(vt_ref[0, j, 0], e.astype(BF16))

    def finish():
        acc = acc_ref[...]
        o_ref[0] = (acc[:MLA_V_DIM] / acc[MLA_V_DIM:MLA_V_DIM + 1]).T.astype(o_ref.dtype)

    st = scores(0)
    m = jnp.max(st[0:MLA_FIRST_ROWS], axis=0, keepdims=True)
    jump = jnp.zeros((1, tq), F32)
    for j in range(nk):
        st_next = scores(j + 1) if j + 1 < nk else None
        mt = jnp.max(st, axis=0, keepdims=True)
        acc = pv(j, jnp.exp2(st - m))
        if j > 0:
            acc = acc_ref[...] + acc
        jump = jnp.maximum(jump, mt - m)
        m_new = jnp.maximum(m, mt)
        acc_ref[...] = acc * jnp.exp2(m - m_new)
        st, m = st_next, m_new
    in_range = jnp.max(jump) <= MLA_JUMP_LIMIT

    @pl.when(in_range)
    def _():
        finish()

    @pl.when(jnp.logical_not(in_range))
    def _():
        acc_ref[...] = jnp.zeros_like(acc_ref)
        st = scores(0)
        m = jnp.full((1, tq), NEG_INF, F32)
        for j in range(nk):
            st_next = scores(j + 1) if j + 1 < nk else None
            m_new = jnp.maximum(m, jnp.max(st, axis=0, keepdims=True))
            acc_ref[...] = jnp.exp2(m - m_new) * acc_ref[...] + pv(j, jnp.exp2(st - m_new))
            st, m = st_next, m_new
        finish()


def _mla_call(qt, kf, vt, *, tq):
    B, H, QP, S = qt.shape
    nk, tk = vt.shape[1], vt.shape[4]
    return pl.pallas_call(
        _mla_kernel,
        out_shape=jax.ShapeDtypeStruct((B, S, H * MLA_V_DIM), BF16),
        grid=(B, H, S // tq),
        in_specs=[
            pl.BlockSpec((1, 1, QP, tq), lambda b, h, i: (b, h, 0, i)),
            pl.BlockSpec((1, 1, S, QP), lambda b, h, i: (b, h, 0, 0)),
            pl.BlockSpec((1, nk, 1, MLA_VT_ROWS, tk), lambda b, h, i: (b, 0, h, 0, 0)),
        ],
        out_specs=pl.BlockSpec((1, tq, MLA_V_DIM), lambda b, h, i: (b, i, h)),
        scratch_shapes=[pltpu.VMEM((MLA_VT_ROWS, tq), F32)],
        compiler_params=pltpu.CompilerParams(
            dimension_semantics=("parallel", "parallel", "parallel"), vmem_limit_bytes=VMEM_LIMIT),
        name="mla",
    )(qt, kf, vt)


def _ffn_kernel(x_ref, xp_ref, xn_ref, oa_ref, oap_ref, oan_ref, ob_ref, obp_ref, obn_ref,
                lneg_ref, lneb_ref, ong_ref, wo_ref, ln1g_ref, ln1b_ref, wup_ref, cw_ref, cb_ref,
                wdn_ref, ln2g_ref, ln2b_ref, out_ref, st, *, alpha, d_ff, n_chunks):
    tm = x_ref.shape[1]
    rows = tm + 2 * HALO
    G = rows // HALO
    i = pl.program_id(1)
    last = pl.num_programs(1) - 1
    hi = BF16_ROWS - HALO
    nslab = x_ref.shape[2] // LANES

    def with_halo(main, prev, nxt):
        return jnp.concatenate([prev, main, nxt], axis=0)

    x = with_halo(x_ref[0], xp_ref[0], xn_ref[0])
    oa = with_halo(oa_ref[0].astype(F32), oap_ref[0].astype(F32)[hi:], oan_ref[0].astype(F32)[:HALO])
    ob = with_halo(ob_ref[0].astype(F32), obp_ref[0].astype(F32)[hi:], obn_ref[0].astype(F32)[:HALO])
    wa = oa.shape[1]
    xn = _layer_norm(x, lneg_ref[...], lneb_ref[...])
    o = jnp.concatenate([_rms_norm(oa, ong_ref[:, :wa]), _rms_norm(ob, ong_ref[:, wa:])], axis=1)
    x1 = _layer_norm(alpha * xn + _dot(o.astype(BF16), wo_ref[...]), ln1g_ref[...], ln1b_ref[...])

    ridx = lax.broadcasted_iota(jnp.int32, (rows, 1), 0)
    inside = jnp.logical_and(jnp.logical_or(ridx >= HALO, i > 0),
                             jnp.logical_or(ridx < tm + HALO, i < last))
    x1m = jnp.where(inside, x1, 0.0)

    for j in range(nslab):
        st[j] = x1m[:, j * LANES:(j + 1) * LANES]
    x1p = jnp.concatenate(
        [jnp.concatenate([st[j, pl.ds(g, HALO, stride=G), :] for j in range(nslab)], axis=1) for g in range(G)],
        axis=0).astype(BF16)

    ck = d_ff // n_chunks

    def up(base):
        return _dot(x1p, wup_ref[:, base:base + ck])

    def conv(u, base):
        before = jnp.concatenate([pltpu.roll(u[rows - HALO:rows], 1, 0), u[0:rows - HALO]], axis=0)
        after = jnp.concatenate([u[HALO:rows], pltpu.roll(u[0:HALO], HALO - 1, 0)], axis=0)
        cw = cw_ref[:, base:base + ck]
        return cb_ref[:, base:base + ck] + before * cw[0:1] + u * cw[1:2] + after * cw[2:3]

    bases = [(c * ck, d_ff + c * ck) for c in range(n_chunks)]
    u_gate, u_val = up(bases[0][0]), up(bases[0][1])
    y = None
    for c in range(n_chunks):
        more = c + 1 < n_chunks
        gate = conv(u_gate, bases[c][0])
        u_gate = up(bases[c + 1][0]) if more else None
        val = conv(u_val, bases[c][1])
        g = (gate / (1.0 + jnp.exp2(gate * (-LOG2E))) * val).astype(BF16)
        u_val = up(bases[c + 1][1]) if more else None
        yc = _dot(g, wdn_ref[c * ck:(c + 1) * ck, :])
        y = yc if y is None else y + yc
    for g in range(G):
        for j in range(nslab):
            st[j, pl.ds(g, HALO, stride=G), :] = y[g * HALO:(g + 1) * HALO, j * LANES:(j + 1) * LANES]
    y = jnp.concatenate([st[j, HALO:HALO + tm, :] for j in range(nslab)], axis=1)
    out_ref[0] = _layer_norm(alpha * x1[HALO:HALO + tm] + y, ln2g_ref[...], ln2b_ref[...])


def _ffn_call(x, o_a, o_b, lne_g, lne_b, on_g, w_o, ln1_g, ln1_b, w_up, conv_w, conv_b, w_down,
              ln2_g, ln2_b, *, tm, alpha, n_chunks):
    B, S, D = x.shape
    d_ff = w_down.shape[0]
    wa, wb = o_a.shape[-1], o_b.shape[-1]
    nt = S // tm
    r8, r16 = tm // HALO, tm // BF16_ROWS
    const = lambda *shape: pl.BlockSpec(shape, lambda b, i: (0,) * len(shape), pipeline_mode=pl.Buffered(1))

    def main(w):
        return pl.BlockSpec((1, tm, w), lambda b, i: (b, i, 0))

    def prev(w, rb, r):
        return pl.BlockSpec((1, rb, w), lambda b, i: (b, jnp.maximum(i * r - 1, 0), 0))

    def nxt(w, rb, r):
        return pl.BlockSpec((1, rb, w), lambda b, i: (b, jnp.minimum((i + 1) * r, S // rb - 1), 0))

    return pl.pallas_call(
        functools.partial(_ffn_kernel, alpha=alpha, d_ff=d_ff, n_chunks=n_chunks),
        out_shape=jax.ShapeDtypeStruct((B, S, D), F32),
        grid=(B, nt),
        in_specs=[
            main(D), prev(D, HALO, r8), nxt(D, HALO, r8),
            main(wa), prev(wa, BF16_ROWS, r16), nxt(wa, BF16_ROWS, r16),
            main(wb), prev(wb, BF16_ROWS, r16), nxt(wb, BF16_ROWS, r16),
            const(*lne_g.shape), const(*lne_b.shape), const(*on_g.shape), const(*w_o.shape),
            const(*ln1_g.shape), const(*ln1_b.shape), const(*w_up.shape), const(*conv_w.shape),
            const(*conv_b.shape), const(*w_down.shape), const(*ln2_g.shape), const(*ln2_b.shape),
        ],
        out_specs=pl.BlockSpec((1, tm, D), lambda b, i: (b, i, 0)),
        scratch_shapes=[pltpu.VMEM((D // LANES, tm + 2 * HALO, LANES), F32)],
        compiler_params=pltpu.CompilerParams(
            dimension_semantics=("parallel", "parallel"), vmem_limit_bytes=VMEM_LIMIT),
        name="ffn",
    )(x, x, x, o_a, o_a, o_a, o_b, o_b, o_b, lne_g, lne_b, on_g, w_o, ln1_g, ln1_b, w_up, conv_w, conv_b,
      w_down, ln2_g, ln2_b)


def _layer(x, positions, ln_emb_g, ln_emb_b, w_in, q_norm_g, w_uq, kv_norm_g, w_ukv, out_norm_g, w_o,
           ln1_g, ln1_b, w_up, conv_w, conv_b, w_down, ln2_g, ln2_b, *, alpha, tm_proj, tq, tm_ffn, n_chunks):
    B, S, D = x.shape
    row = lambda v: v.reshape(1, -1).astype(F32)
    w_in_p = jnp.concatenate([w_in, jnp.zeros((D, LANES - MLA_ROPE_DIM), w_in.dtype)], axis=1).astype(BF16)
    q_rank = w_uq.shape[0]
    wq = w_uq.reshape(q_rank, MLA_HEADS, MLA_QK_DIM)
    wq = jnp.pad(wq, ((0, 0), (0, 0), (0, MLA_QK_PAD - MLA_QK_DIM)))
    wuqt = wq.reshape(q_rank, MLA_HEADS * MLA_QK_PAD).T.astype(BF16)
    kv_rank = w_ukv.shape[0]
    wkv = w_ukv.reshape(kv_rank, MLA_HEADS, MLA_NOPE_DIM + MLA_V_DIM)
    wuk = wkv[:, :, :MLA_NOPE_DIM].reshape(kv_rank, MLA_HEADS * MLA_NOPE_DIM).astype(BF16)
    wuvt = wkv[:, :, MLA_NOPE_DIM:].reshape(kv_rank, MLA_HEADS * MLA_V_DIM).T.astype(BF16)

    def inv_freq(rot_dim):
        half = rot_dim // 2
        return jnp.power(jnp.float32(ROPE_THETA),
                         -jnp.arange(half, dtype=F32) * (2.0 / rot_dim)).reshape(half, 1)

    pos_row = positions.astype(F32).reshape(B, 1, S)
    q_a, k_a, v_a, qt, kf, vt = _proj_call(
        x, pos_row, inv_freq(SWA_ROT_DIM), inv_freq(MLA_ROPE_DIM), row(ln_emb_g), row(ln_emb_b), w_in_p,
        row(q_norm_g), row(kv_norm_g), wuqt, wuk, wuvt, tm=tm_proj)
    o_a = _swa_call(q_a, k_a, v_a)
    o_b = _mla_call(qt, kf, vt, tq=tq)
    return _ffn_call(x, o_a, o_b, row(ln_emb_g), row(ln_emb_b), row(out_norm_g), w_o.astype(BF16),
                     row(ln1_g), row(ln1_b), w_up.astype(BF16), conv_w.astype(F32), row(conv_b),
                     w_down.astype(BF16), row(ln2_g), row(ln2_b), tm=tm_ffn, alpha=alpha, n_chunks=n_chunks)


def kernel(x, positions, ln_emb_g, ln_emb_b, w_in, q_norm_g, w_uq, kv_norm_g, w_ukv, out_norm_g, w_o,
           ln1_g, ln1_b, w_up, conv_w, conv_b, w_down, ln2_g, ln2_b):
    depth = w_in.shape[0]
    assert depth == 1, "single-layer stack"
    alpha = (2.0 * depth) ** 0.25
    S = x.shape[1]
    tile = min(512, S)
    return _layer(x, positions, ln_emb_g, ln_emb_b, w_in[0], q_norm_g[0], w_uq[0], kv_norm_g[0], w_ukv[0],
                  out_norm_g[0], w_o[0], ln1_g[0], ln1_b[0], w_up[0], conv_w[0], conv_b[0], w_down[0],
                  ln2_g[0], ln2_b[0], alpha=alpha, tm_proj=tile, tq=tile, tm_ffn=tile, n_chunks=2)
```

```python
import functools
import math

import jax
import jax.numpy as jnp
from jax import lax
from jax.experimental import pallas as pl
from jax.experimental.pallas import tpu as pltpu

F32 = jnp.float32
BF16 = jnp.bfloat16

SWA_HEAD_DIM = 64
SWA_PATTERNS = ((128, 1), (512, 4), (2048, 16))
SWA_ROT_DIM = 16
MLA_NOPE_DIM = 128
MLA_ROPE_DIM = 64
MLA_V_DIM = 128
MLA_QK_DIM = MLA_NOPE_DIM + MLA_ROPE_DIM
MLA_HEADS = 4
MLA_QK_PAD = 256
ROPE_THETA = 500000.0
LN_EPS = 1e-5
RMS_EPS = 1e-6
NEG_INF = -1e30
CONV_WIDTH = 3
LOG2E = math.log2(math.e)

LANES = 128
HALO = 8
BF16_ROWS = 16
MLA_VT_ROWS = MLA_V_DIM + BF16_ROWS
MLA_JUMP_LIMIT = 60.0
MLA_FIRST_ROWS = 128
VMEM_LIMIT = 56 * 1024 * 1024
FFN_VMEM_LIMIT = 62 * 1024 * 1024


def _dot(a, b):
    return jnp.dot(a, b, preferred_element_type=F32)


def _dot_nt(a, b):
    return lax.dot_general(a, b, (((1,), (1,)), ((), ())), preferred_element_type=F32)


def _layer_norm(x, g, b):
    mu = jnp.mean(x, axis=-1, keepdims=True)
    xc = x - mu
    var = jnp.mean(xc * xc, axis=-1, keepdims=True)
    return xc * lax.rsqrt(var + LN_EPS) * g + b


def _rms_norm(x, g):
    return x * lax.rsqrt(jnp.mean(x * x, axis=-1, keepdims=True) + RMS_EPS) * g


def _proj_kernel(x_ref, pos_ref, invfa_ref, invfm_ref, lng_ref, lnb_ref, win_ref, qng_ref, kvng_ref,
                 wuqt_ref, wuk_ref, wuvt_ref,
                 qa_ref, ka_ref, va_ref, qt_ref, kf_ref, vt_ref, *, swa_w, q_rank, kv_rank):
    tm = x_ref.shape[1]
    xn = _layer_norm(x_ref[0], lng_ref[...], lnb_ref[...]).astype(BF16)
    o_cq = 3 * swa_w
    o_ckv = o_cq + q_rank
    o_kr = o_ckv + kv_rank

    def in_proj(c0, c1):
        return _dot(xn, win_ref[:, c0:c1])

    h_b = in_proj(o_cq, o_kr + LANES)
    h_q = in_proj(0, swa_w)

    pos = pos_ref[0]
    cqn = _rms_norm(h_b[:, 0:q_rank], qng_ref[...]).astype(BF16)
    ckvn = _rms_norm(h_b[:, q_rank:q_rank + kv_rank], kvng_ref[...]).astype(BF16)
    ang_m = invfm_ref[...] * pos
    cos_m, sin_m = jnp.cos(ang_m), jnp.sin(ang_m)
    hm = MLA_ROPE_DIM // 2

    def rope_t(x1, x2):
        return x1 * cos_m - x2 * sin_m, x2 * cos_m + x1 * sin_m

    qt = _dot_nt(wuqt_ref[...], cqn) * (MLA_QK_DIM ** -0.5 * LOG2E)
    for hd in range(MLA_HEADS):
        r0 = hd * MLA_QK_PAD
        qt_ref[0, hd, 0:MLA_NOPE_DIM, :] = qt[r0:r0 + MLA_NOPE_DIM].astype(BF16)
        p0 = r0 + MLA_NOPE_DIM
        n1, n2 = rope_t(qt[p0:p0 + hm], qt[p0 + hm:p0 + 2 * hm])
        qt_ref[0, hd, MLA_NOPE_DIM:MLA_NOPE_DIM + hm, :] = n1.astype(BF16)
        qt_ref[0, hd, MLA_NOPE_DIM + hm:MLA_QK_DIM, :] = n2.astype(BF16)
        qt_ref[0, hd, MLA_QK_DIM:MLA_QK_PAD, :] = jnp.zeros((MLA_QK_PAD - MLA_QK_DIM, tm), BF16)

    kr_t = h_b[:, q_rank + kv_rank:].T
    k1, k2 = rope_t(kr_t[0:hm], kr_t[hm:2 * hm])
    kpe = jnp.concatenate([k1, k2, kr_t[2 * hm:]], axis=0).T.astype(BF16)
    kn = _dot(ckvn, wuk_ref[...])
    for hd in range(MLA_HEADS):
        kf_ref[0, hd, :, 0:MLA_NOPE_DIM] = kn[:, hd * MLA_NOPE_DIM:(hd + 1) * MLA_NOPE_DIM].astype(BF16)
        kf_ref[0, hd, :, MLA_NOPE_DIM:MLA_QK_PAD] = kpe
    vt = _dot_nt(wuvt_ref[...], ckvn)
    for hd in range(MLA_HEADS):
        vt_ref[0, 0, hd, 0:MLA_V_DIM, :] = vt[hd * MLA_V_DIM:(hd + 1) * MLA_V_DIM].astype(BF16)
        vt_ref[0, 0, hd, MLA_V_DIM:MLA_VT_ROWS, :] = jnp.ones((MLA_VT_ROWS - MLA_V_DIM, tm), BF16)

    h_k = in_proj(swa_w, 2 * swa_w)
    ang_a = invfa_ref[...] * pos
    cos_a, sin_a = jnp.cos(ang_a), jnp.sin(ang_a)
    half = SWA_ROT_DIM // 2
    rest = SWA_HEAD_DIM - SWA_ROT_DIM
    ones = jnp.ones((rest, tm), F32)
    zeros = jnp.zeros((rest, tm), F32)
    zhalf = jnp.zeros((half, tm), F32)
    reps = LANES // SWA_HEAD_DIM
    cos_t = jnp.concatenate([cos_a, cos_a, ones] * reps, axis=0).T
    sin_lo = jnp.concatenate([-sin_a, zhalf, zeros] * reps, axis=0).T
    sin_hi = jnp.concatenate([zhalf, sin_a, zeros] * reps, axis=0).T

    def rope_a(xs):
        return (xs * cos_t + pltpu.roll(xs, LANES - half, 1) * sin_lo + pltpu.roll(xs, half, 1) * sin_hi)

    qscale = SWA_HEAD_DIM ** -0.5 * LOG2E
    for j in range(swa_w // LANES):
        c0 = j * LANES
        qa_ref[0, :, c0:c0 + LANES] = (rope_a(h_q[:, c0:c0 + LANES]) * qscale).astype(BF16)
    h_v = in_proj(2 * swa_w, 3 * swa_w)
    for j in range(swa_w // LANES):
        c0 = j * LANES
        ka_ref[0, :, c0:c0 + LANES] = rope_a(h_k[:, c0:c0 + LANES]).astype(BF16)
    va_ref[0] = h_v.astype(BF16)


def _proj_call(x, pos_row, invf_a, invf_m, ln_g, ln_b, w_in_p, qn_g, kvn_g, wuqt, wuk, wuvt, *, tm):
    B, S, D = x.shape
    swa_w = 512
    q_rank = qn_g.shape[-1]
    kv_rank = kvn_g.shape[-1]
    nt = S // tm
    const = lambda *shape: pl.BlockSpec(shape, lambda b, i: (0,) * len(shape))
    out_shape = (
        jax.ShapeDtypeStruct((B, S, swa_w), BF16),
        jax.ShapeDtypeStruct((B, S, swa_w), BF16),
        jax.ShapeDtypeStruct((B, S, swa_w), BF16),
        jax.ShapeDtypeStruct((B, MLA_HEADS, MLA_QK_PAD, S), BF16),
        jax.ShapeDtypeStruct((B, MLA_HEADS, S, MLA_QK_PAD), BF16),
        jax.ShapeDtypeStruct((B, nt, MLA_HEADS, MLA_VT_ROWS, tm), BF16),
    )
    return pl.pallas_call(
        functools.partial(_proj_kernel, swa_w=swa_w, q_rank=q_rank, kv_rank=kv_rank),
        out_shape=out_shape,
        grid=(B, nt),
        in_specs=[
            pl.BlockSpec((1, tm, D), lambda b, i: (b, i, 0)),
            pl.BlockSpec((1, 1, tm), lambda b, i: (b, 0, i)),
            const(*invf_a.shape), const(*invf_m.shape), const(*ln_g.shape), const(*ln_b.shape),
            const(*w_in_p.shape), const(*qn_g.shape), const(*kvn_g.shape),
            const(*wuqt.shape), const(*wuk.shape), const(*wuvt.shape),
        ],
        out_specs=(
            pl.BlockSpec((1, tm, swa_w), lambda b, i: (b, i, 0)),
            pl.BlockSpec((1, tm, swa_w), lambda b, i: (b, i, 0)),
            pl.BlockSpec((1, tm, swa_w), lambda b, i: (b, i, 0)),
            pl.BlockSpec((1, MLA_HEADS, MLA_QK_PAD, tm), lambda b, i: (b, 0, 0, i)),
            pl.BlockSpec((1, MLA_HEADS, tm, MLA_QK_PAD), lambda b, i: (b, 0, i, 0)),
            pl.BlockSpec((1, 1, MLA_HEADS, MLA_VT_ROWS, tm), lambda b, i: (b, i, 0, 0, 0)),
        ),
        compiler_params=pltpu.CompilerParams(
            dimension_semantics=("parallel", "parallel"), vmem_limit_bytes=VMEM_LIMIT),
        name="proj",
    )(x, pos_row, invf_a, invf_m, ln_g, ln_b, w_in_p, qn_g, kvn_g, wuqt, wuk, wuvt)


def _swa_geometry(S, window, d, bq_max):
    L = S // d
    n_side = window // (2 * d)
    bq = min(bq_max, L)
    win = min(L, bq + 2 * n_side)
    return L, n_side, bq, win, L // bq


def _swa_kernel(q_ref, k_ref, v_ref, o_ref, stage, qd, kd, vd, bias, op, lp, ot, lt, *, S, bq_max, unroll):
    lane = lax.broadcasted_iota(jnp.int32, (1, LANES), 1)
    head0 = lane < SWA_HEAD_DIM

    dils = [d for _, d in SWA_PATTERNS if d != 1]
    for src, dst in ((q_ref, qd), (k_ref, kd), (v_ref, vd)):
        stage[0] = src[0].astype(F32)
        cur, d_prev = 0, 1
        for pi, d in enumerate(dils):
            f, L, Lp = d // d_prev, S // d, S // d_prev
            for rp in range(d_prev):
                for q in range(f):
                    r = q * d_prev + rp
                    t = stage[cur, pl.ds(rp * Lp + q, L, stride=f), :]
                    dst[pi, r * L:(r + 1) * L, :] = t.astype(BF16)
                    if pi + 1 < len(dils):
                        stage[1 - cur, r * L:(r + 1) * L, :] = t
            cur, d_prev = 1 - cur, d

    for p, (window, d) in enumerate(SWA_PATTERNS):
        L, n_side, bq, win, nb = _swa_geometry(S, window, d, bq_max)
        rel = (lax.broadcasted_iota(jnp.int32, (bq, win), 0) - lax.broadcasted_iota(jnp.int32, (bq, win), 1))
        ws_last = min(max((nb - 1) * bq - n_side, 0), L - win)
        for c, delta in enumerate((0, n_side, (nb - 1) * bq - ws_last)):
            bias[p, c, 0:bq, 0:win] = jnp.where(jnp.abs(rel + delta) <= n_side, 0.0, NEG_INF)

    pi = 0
    for p, (window, d) in enumerate(SWA_PATTERNS):
        L, n_side, bq, win, nb = _swa_geometry(S, window, d, bq_max)
        if d == 1:
            load_q = lambda r0, n: q_ref[0, pl.ds(r0, n), :]
            load_k = lambda r0, n: k_ref[0, pl.ds(r0, n), :]
            load_v = lambda r0, n: v_ref[0, pl.ds(r0, n), :]
        else:
            load_q = functools.partial(lambda r0, n, j: qd[j, pl.ds(r0, n), :], j=pi)
            load_k = functools.partial(lambda r0, n, j: kd[j, pl.ds(r0, n), :], j=pi)
            load_v = functools.partial(lambda r0, n, j: vd[j, pl.ds(r0, n), :], j=pi)
            pi += 1

        def body(n, carry, L=L, n_side=n_side, bq=bq, win=win, nb=nb, p=p,
                 load_q=load_q, load_k=load_k, load_v=load_v):
            r = n // nb
            i = n % nb
            row0 = pl.multiple_of(r * L + i * bq, bq)
            ws = jnp.clip(i * bq - n_side, 0, L - win)
            krow0 = pl.multiple_of(r * L + ws, BF16_ROWS)
            q = load_q(row0, bq)
            k = load_k(krow0, win)
            v1 = jnp.concatenate([load_v(krow0, win), jnp.ones((win, LANES), BF16)], axis=1)
            mask = bias[p, jnp.where(i == 0, 0, jnp.where(i == nb - 1, 2, 1)), 0:bq, 0:win]
            outs, lses = [], []
            for hsel in (head0, jnp.logical_not(head0)):
                qh = jnp.where(hsel, q, jnp.zeros_like(q))
                s = _dot_nt(qh, k) + mask
                m = jnp.max(s, axis=1, keepdims=True)
                e = jnp.exp2(s - m).astype(BF16)
                oa = _dot(e, v1)
                den = oa[:, LANES:]
                outs.append(oa[:, :LANES] / den)
                lses.append(m + jnp.log2(den))
            op[p, pl.ds(row0, bq), :] = jnp.where(head0, outs[0], outs[1])
            lp[p, pl.ds(row0, bq), :] = jnp.where(head0, lses[0], lses[1])
            return carry

        lax.fori_loop(0, S // bq, body, 0, unroll=unroll)

    def reader(ref, lead):
        return lambda rows: ref[lead, rows, :]

    def writer(ref, lead):
        def write(rows, val):
            ref[lead, rows, :] = val
        return write

    def spread(read, write, d, d_prev):
        f, L, Lp = d // d_prev, S // d, S // d_prev
        for rp in range(d_prev):
            for q in range(f):
                r = q * d_prev + rp
                write(pl.ds(rp * Lp + q, L, stride=f), read(slice(r * L, (r + 1) * L)))

    pi = 0
    for p, (_, d) in enumerate(SWA_PATTERNS):
        if d == 1:
            continue
        chain = [1] + dils[:pi + 1]
        for src, out in ((op, ot), (lp, lt)):
            read = reader(src, p)
            for step in range(pi, -1, -1):
                write = writer(out, pi) if step == 0 else writer(stage, step % 2)
                spread(read, write, chain[step + 1], chain[step])
                read = reader(stage, step % 2)
        pi += 1

    cr = min(512, S)

    def combine(c, carry):
        r0 = pl.multiple_of(c * cr, cr)
        os_, ls_ = [], []
        pj = 0
        for p, (_, d) in enumerate(SWA_PATTERNS):
            if d == 1:
                os_.append(op[p, pl.ds(r0, cr), :])
                ls_.append(lp[p, pl.ds(r0, cr), :])
            else:
                os_.append(ot[pj, pl.ds(r0, cr), :])
                ls_.append(lt[pj, pl.ds(r0, cr), :])
                pj += 1
        m = functools.reduce(jnp.maximum, ls_)
        es = [jnp.exp2(l - m) for l in ls_]
        den = functools.reduce(lambda a, b: a + b, es)
        num = functools.reduce(lambda a, b: a + b, [e * o for e, o in zip(es, os_)])
        o_ref[0, pl.ds(r0, cr), :] = (num / den).astype(o_ref.dtype)
        return carry

    lax.fori_loop(0, S // cr, combine, 0)


def _swa_call(q_a, k_a, v_a):
    B, S, W = q_a.shape
    npat = len(SWA_PATTERNS)
    nd = sum(1 for _, d in SWA_PATTERNS if d != 1)
    bq_max = 128
    geo = [_swa_geometry(S, w, d, bq_max) for w, d in SWA_PATTERNS]
    bq, win = max(g[2] for g in geo), max(g[3] for g in geo)
    spec = pl.BlockSpec((1, S, LANES), lambda b, j: (b, 0, j))
    return pl.pallas_call(
        functools.partial(_swa_kernel, S=S, bq_max=bq_max, unroll=16),
        out_shape=jax.ShapeDtypeStruct((B, S, W), BF16),
        grid=(B, W // LANES),
        in_specs=[spec, spec, spec],
        out_specs=spec,
        scratch_shapes=[
            pltpu.VMEM((2, S, LANES), F32),
            pltpu.VMEM((nd, S, LANES), BF16),
            pltpu.VMEM((nd, S, LANES), BF16),
            pltpu.VMEM((nd, S, LANES), BF16),
            pltpu.VMEM((npat, 3, bq, win), F32),
            pltpu.VMEM((npat, S, LANES), F32),
            pltpu.VMEM((npat, S, LANES), F32),
            pltpu.VMEM((nd, S, LANES), F32),
            pltpu.VMEM((nd, S, LANES), F32),
        ],
        compiler_params=pltpu.CompilerParams(
            dimension_semantics=("parallel", "parallel"), vmem_limit_bytes=VMEM_LIMIT),
        name="swa",
    )(q_a, k_a, v_a)


def _mla_kernel(qt_ref, k_ref, vt_ref, o_ref, acc_ref, *, width):
    ngroups = qt_ref.shape[3] // width
    nk, tk = vt_ref.shape[1], vt_ref.shape[4]

    def scores(j, qt):
        return _dot(k_ref[0, 0, j * tk:(j + 1) * tk, :], qt)

    def pv(j, e):
        return _dot(vt_ref[0, j, 0], e.astype(BF16))

    def finish(c):
        acc = acc_ref[c]
        o_ref[0, c * width:(c + 1) * width, :] = (
            acc[:MLA_V_DIM] / acc[MLA_V_DIM:MLA_V_DIM + 1]).T.astype(o_ref.dtype)

    jump = jnp.zeros((1, width), F32)
    for c in range(ngroups):
        qt = qt_ref[0, 0, :, c * width:(c + 1) * width]
        st = scores(0, qt)
        m = jnp.max(st[0:MLA_FIRST_ROWS], axis=0, keepdims=True)
        for j in range(nk):
            st_next = scores(j + 1, qt) if j + 1 < nk else None
            mt = jnp.max(st, axis=0, keepdims=True)
            acc = pv(j, jnp.exp2(st - m))
            if j > 0:
                acc = acc_ref[c] + acc
            jump = jnp.maximum(jump, mt - m)
            m_new = jnp.maximum(m, mt)
            acc_ref[c] = acc * jnp.exp2(m - m_new)
            st, m = st_next, m_new
        finish(c)

    @pl.when(jnp.max(jump) > MLA_JUMP_LIMIT)
    def _():
        for c in range(ngroups):
            qt = qt_ref[0, 0, :, c * width:(c + 1) * width]
            acc_ref[c] = jnp.zeros(acc_ref.shape[1:], F32)
            st = scores(0, qt)
            m = jnp.full((1, width), NEG_INF, F32)
            for j in range(nk):
                st_next = scores(j + 1, qt) if j + 1 < nk else None
                m_new = jnp.maximum(m, jnp.max(st, axis=0, keepdims=True))
                acc_ref[c] = jnp.exp2(m - m_new) * acc_ref[c] + pv(j, jnp.exp2(st - m_new))
                st, m = st_next, m_new
            finish(c)


def _mla_call(qt, kf, vt, *, tq, width):
    B, H, QP, S = qt.shape
    nk, tk = vt.shape[1], vt.shape[4]
    return pl.pallas_call(
        functools.partial(_mla_kernel, width=width),
        out_shape=jax.ShapeDtypeStruct((B, S, H * MLA_V_DIM), BF16),
        grid=(B, H, S // tq),
        in_specs=[
            pl.BlockSpec((1, 1, QP, tq), lambda b, h, i: (b, h, 0, i)),
            pl.BlockSpec((1, 1, S, QP), lambda b, h, i: (b, h, 0, 0)),
            pl.BlockSpec((1, nk, 1, MLA_VT_ROWS, tk), lambda b, h, i: (b, 0, h, 0, 0)),
        ],
        out_specs=pl.BlockSpec((1, tq, MLA_V_DIM), lambda b, h, i: (b, i, h)),
        scratch_shapes=[pltpu.VMEM((tq // width, MLA_VT_ROWS, width), F32)],
        compiler_params=pltpu.CompilerParams(
            dimension_semantics=("parallel", "parallel", "parallel"), vmem_limit_bytes=VMEM_LIMIT),
        name="mla",
    )(qt, kf, vt)


def _ffn_kernel(x_ref, xp_ref, xn_ref, oa_ref, oap_ref, oan_ref, ob_ref, obp_ref, obn_ref,
                lneg_ref, lneb_ref, ong_ref, wo_ref, ln1g_ref, ln1b_ref, wup_ref, cw_ref, cb_ref,
                wdn_ref, ln2g_ref, ln2b_ref, out_ref, st_all, *, alpha, d_ff, n_chunks, tm):
    n_tiles = x_ref.shape[1] // tm
    rows = tm + 2 * HALO
    G = rows // HALO
    i = pl.program_id(1)
    last = pl.num_programs(1) - 1
    nslab = x_ref.shape[2] // LANES
    ck = d_ff // n_chunks

    def head(t):
        r0, r1 = t * tm, (t + 1) * tm
        first_tile, last_tile = t == 0, t == n_tiles - 1

        def with_halo(main, prev_block, next_block, width):
            prev = (prev_block[0].astype(F32)[width - HALO:] if first_tile
                    else main[0, r0 - width:r0, :].astype(F32)[width - HALO:])
            nxt = (next_block[0].astype(F32)[:HALO] if last_tile
                   else main[0, r1:r1 + width, :].astype(F32)[:HALO])
            return jnp.concatenate([prev, main[0, r0:r1, :].astype(F32), nxt], axis=0)

        x = with_halo(x_ref, xp_ref, xn_ref, HALO)
        oa = with_halo(oa_ref, oap_ref, oan_ref, BF16_ROWS)
        ob = with_halo(ob_ref, obp_ref, obn_ref, BF16_ROWS)
        wa = oa.shape[1]
        xn = _layer_norm(x, lneg_ref[...], lneb_ref[...])
        o = jnp.concatenate([_rms_norm(oa, ong_ref[:, :wa]), _rms_norm(ob, ong_ref[:, wa:])], axis=1)
        x1 = _layer_norm(alpha * xn + _dot(o.astype(BF16), wo_ref[...]), ln1g_ref[...], ln1b_ref[...])

        ridx = lax.broadcasted_iota(jnp.int32, (rows, 1), 0)
        inside = jnp.logical_and(jnp.logical_or(ridx >= HALO, jnp.logical_or(i > 0, not first_tile)),
                                 jnp.logical_or(ridx < tm + HALO, jnp.logical_or(i < last, not last_tile)))
        x1m = jnp.where(inside, x1, 0.0)

        st = st_all.at[t]
        for j in range(nslab):
            st[j] = x1m[:, j * LANES:(j + 1) * LANES]
        x1p = jnp.concatenate(
            [jnp.concatenate([st[j, pl.ds(g, HALO, stride=G), :] for j in range(nslab)], axis=1)
             for g in range(G)], axis=0).astype(BF16)
        return x1[HALO:HALO + tm], x1p

    def conv(u, base):
        before = jnp.concatenate([pltpu.roll(u[rows - HALO:rows], 1, 0), u[0:rows - HALO]], axis=0)
        after = jnp.concatenate([u[HALO:rows], pltpu.roll(u[0:HALO], HALO - 1, 0)], axis=0)
        cw = cw_ref[:, base:base + ck]
        return cb_ref[:, base:base + ck] + before * cw[0:1] + u * cw[1:2] + after * cw[2:3]

    def body(x1p, emit_between):
        def up(base):
            return _dot(x1p, wup_ref[:, base:base + ck])

        bases = [(c * ck, d_ff + c * ck) for c in range(n_chunks)]
        u_gate, u_val = up(bases[0][0]), up(bases[0][1])
        between = emit_between()
        y = None
        for c in range(n_chunks):
            more = c + 1 < n_chunks
            gate = conv(u_gate, bases[c][0])
            u_gate = up(bases[c + 1][0]) if more else None
            val = conv(u_val, bases[c][1])
            g = (gate / (1.0 + jnp.exp2(gate * (-LOG2E))) * val).astype(BF16)
            u_val = up(bases[c + 1][1]) if more else None
            yc = _dot(g, wdn_ref[c * ck:(c + 1) * ck, :])
            y = yc if y is None else y + yc
        return y, between

    def tail(t, x1_main, y):
        st = st_all.at[t]
        for g in range(G):
            for j in range(nslab):
                st[j, pl.ds(g, HALO, stride=G), :] = y[g * HALO:(g + 1) * HALO, j * LANES:(j + 1) * LANES]
        y = jnp.concatenate([st[j, HALO:HALO + tm, :] for j in range(nslab)], axis=1)
        out_ref[0, t * tm:(t + 1) * tm, :] = _layer_norm(alpha * x1_main + y, ln2g_ref[...], ln2b_ref[...])

    x1_main, x1p = head(0)
    pending = None
    for t in range(n_tiles):
        def emit_between(t=t, pending=pending):
            if pending is not None:
                tail(*pending)
            return head(t + 1) if t + 1 < n_tiles else None
        y, nxt = body(x1p, emit_between)
        pending = (t, x1_main, y)
        if nxt is not None:
            x1_main, x1p = nxt
    tail(*pending)


def _ffn_call(x, o_a, o_b, lne_g, lne_b, on_g, w_o, ln1_g, ln1_b, w_up, conv_w, conv_b, w_down,
              ln2_g, ln2_b, *, tm, tiles_per_step, alpha, n_chunks):
    B, S, D = x.shape
    d_ff = w_down.shape[0]
    wa, wb = o_a.shape[-1], o_b.shape[-1]
    bm = tm * tiles_per_step
    nt = S // bm
    r8, r16 = bm // HALO, bm // BF16_ROWS
    const = lambda *shape: pl.BlockSpec(shape, lambda b, i: (0,) * len(shape), pipeline_mode=pl.Buffered(1))

    def main(w):
        return pl.BlockSpec((1, bm, w), lambda b, i: (b, i, 0))

    def prev(w, rb, r):
        return pl.BlockSpec((1, rb, w), lambda b, i: (b, jnp.maximum(i * r - 1, 0), 0))

    def nxt(w, rb, r):
        return pl.BlockSpec((1, rb, w), lambda b, i: (b, jnp.minimum((i + 1) * r, S // rb - 1), 0))

    return pl.pallas_call(
        functools.partial(_ffn_kernel, alpha=alpha, d_ff=d_ff, n_chunks=n_chunks, tm=tm),
        out_shape=jax.ShapeDtypeStruct((B, S, D), F32),
        grid=(B, nt),
        in_specs=[
            main(D), prev(D, HALO, r8), nxt(D, HALO, r8),
            main(wa), prev(wa, BF16_ROWS, r16), nxt(wa, BF16_ROWS, r16),
            main(wb), prev(wb, BF16_ROWS, r16), nxt(wb, BF16_ROWS, r16),
            const(*lne_g.shape), const(*lne_b.shape), const(*on_g.shape), const(*w_o.shape),
            const(*ln1_g.shape), const(*ln1_b.shape), const(*w_up.shape), const(*conv_w.shape),
            const(*conv_b.shape), const(*w_down.shape), const(*ln2_g.shape), const(*ln2_b.shape),
        ],
        out_specs=pl.BlockSpec((1, bm, D), lambda b, i: (b, i, 0)),
        scratch_shapes=[pltpu.VMEM((tiles_per_step, D // LANES, tm + 2 * HALO, LANES), F32)],
        compiler_params=pltpu.CompilerParams(
            dimension_semantics=("parallel", "parallel"), vmem_limit_bytes=FFN_VMEM_LIMIT),
        name="ffn",
    )(x, x, x, o_a, o_a, o_a, o_b, o_b, o_b, lne_g, lne_b, on_g, w_o, ln1_g, ln1_b, w_up, conv_w, conv_b,
      w_down, ln2_g, ln2_b)


def _layer(x, positions, ln_emb_g, ln_emb_b, w_in, q_norm_g, w_uq, kv_norm_g, w_ukv, out_norm_g, w_o,
           ln1_g, ln1_b, w_up, conv_w, conv_b, w_down, ln2_g, ln2_b, *, alpha, tm_proj, tq, tm_ffn, n_chunks):
    B, S, D = x.shape
    row = lambda v: v.reshape(1, -1).astype(F32)
    w_in_p = jnp.concatenate([w_in, jnp.zeros((D, LANES - MLA_ROPE_DIM), w_in.dtype)], axis=1).astype(BF16)
    q_rank = w_uq.shape[0]
    wq = w_uq.reshape(q_rank, MLA_HEADS, MLA_QK_DIM)
    wq = jnp.pad(wq, ((0, 0), (0, 0), (0, MLA_QK_PAD - MLA_QK_DIM)))
    wuqt = wq.reshape(q_rank, MLA_HEADS * MLA_QK_PAD).T.astype(BF16)
    kv_rank = w_ukv.shape[0]
    wkv = w_ukv.reshape(kv_rank, MLA_HEADS, MLA_NOPE_DIM + MLA_V_DIM)
    wuk = wkv[:, :, :MLA_NOPE_DIM].reshape(kv_rank, MLA_HEADS * MLA_NOPE_DIM).astype(BF16)
    wuvt = wkv[:, :, MLA_NOPE_DIM:].reshape(kv_rank, MLA_HEADS * MLA_V_DIM).T.astype(BF16)

    def inv_freq(rot_dim):
        half = rot_dim // 2
        return jnp.power(jnp.float32(ROPE_THETA),
                         -jnp.arange(half, dtype=F32) * (2.0 / rot_dim)).reshape(half, 1)

    pos_row = positions.astype(F32).reshape(B, 1, S)
    q_a, k_a, v_a, qt, kf, vt = _proj_call(
        x, pos_row, inv_freq(SWA_ROT_DIM), inv_freq(MLA_ROPE_DIM), row(ln_emb_g), row(ln_emb_b), w_in_p,
        row(q_norm_g), row(kv_norm_g), wuqt, wuk, wuvt, tm=tm_proj)
    o_a = _swa_call(q_a, k_a, v_a)
    o_b = _mla_call(qt, kf, vt, tq=min(4 * tq, S), width=tq)
    return _ffn_call(x, o_a, o_b, row(ln_emb_g), row(ln_emb_b), row(out_norm_g), w_o.astype(BF16),
                     row(ln1_g), row(ln1_b), w_up.astype(BF16), conv_w.astype(F32), row(conv_b),
                     w_down.astype(BF16), row(ln2_g), row(ln2_b), tm=tm_ffn,
                     tiles_per_step=2 if S % (2 * tm_ffn) == 0 else 1, alpha=alpha, n_chunks=n_chunks)


def kernel(x, positions, ln_emb_g, ln_emb_b, w_in, q_norm_g, w_uq, kv_norm_g, w_ukv, out_norm_g, w_o,
           ln1_g, ln1_b, w_up, conv_w, conv_b, w_down, ln2_g, ln2_b):
    depth = w_in.shape[0]
    assert depth == 1, "single-layer stack"
    alpha = (2.0 * depth) ** 0.25
    S = x.shape[1]
    tile = min(512, S)
    return _layer(x, positions, ln_emb_g, ln_emb_b, w_in[0], q_norm_g[0], w_uq[0], kv_norm_g[0], w_ukv[0],
                  out_norm_g[0], w_o[0], ln1_g[0], ln1_b[0], w_up[0], conv_w[0], conv_b[0], w_down[0],
                  ln2_g[0], ln2_b[0], alpha=alpha, tm_proj=tile, tq=tile, tm_ffn=tile, n_chunks=2)
```

```python
import functools
import math

import jax
import jax.numpy as jnp
from jax import lax
from jax.experimental import pallas as pl
from jax.experimental.pallas import tpu as pltpu

F32 = jnp.float32
BF16 = jnp.bfloat16

SWA_HEAD_DIM = 64
SWA_PATTERNS = ((128, 1), (512, 4), (2048, 16))
SWA_ROT_DIM = 16
MLA_NOPE_DIM = 128
MLA_ROPE_DIM = 64
MLA_V_DIM = 128
MLA_QK_DIM = MLA_NOPE_DIM + MLA_ROPE_DIM
MLA_HEADS = 4
MLA_QK_PAD = 256
ROPE_THETA = 500000.0
LN_EPS = 1e-5
RMS_EPS = 1e-6
NEG_INF = -1e30
CONV_WIDTH = 3
LOG2E = math.log2(math.e)

LANES = 128
HALO = 8
BF16_ROWS = 16
MLA_VT_ROWS = MLA_V_DIM + BF16_ROWS
MLA_JUMP_LIMIT = 60.0
MLA_FIRST_ROWS = 128
VMEM_LIMIT = 56 * 1024 * 1024
FFN_VMEM_LIMIT = 62 * 1024 * 1024


def _dot(a, b):
    return jnp.dot(a, b, preferred_element_type=F32)


def _dot_nt(a, b):
    return lax.dot_general(a, b, (((1,), (1,)), ((), ())), preferred_element_type=F32)


def _layer_norm(x, g, b):
    mu = jnp.mean(x, axis=-1, keepdims=True)
    xc = x - mu
    var = jnp.mean(xc * xc, axis=-1, keepdims=True)
    return xc * lax.rsqrt(var + LN_EPS) * g + b


def _rms_norm(x, g):
    return x * lax.rsqrt(jnp.mean(x * x, axis=-1, keepdims=True) + RMS_EPS) * g


def _proj_kernel(x_ref, pos_ref, invfa_ref, invfm_ref, lng_ref, lnb_ref, win_ref, qng_ref, kvng_ref,
                 wuqt_ref, wuk_ref, wuvt_ref,
                 qa_ref, ka_ref, va_ref, qt_ref, kf_ref, vt_ref, *, swa_w, q_rank, kv_rank):
    tm = x_ref.shape[1]
    xn = _layer_norm(x_ref[0], lng_ref[...], lnb_ref[...]).astype(BF16)
    o_cq = 3 * swa_w
    o_ckv = o_cq + q_rank
    o_kr = o_ckv + kv_rank

    def in_proj(c0, c1):
        return _dot(xn, win_ref[:, c0:c1])

    h_b = in_proj(o_cq, o_kr + LANES)
    h_q = in_proj(0, swa_w)

    pos = pos_ref[0]
    cqn = _rms_norm(h_b[:, 0:q_rank], qng_ref[...]).astype(BF16)
    ckvn = _rms_norm(h_b[:, q_rank:q_rank + kv_rank], kvng_ref[...]).astype(BF16)
    ang_m = invfm_ref[...] * pos
    cos_m, sin_m = jnp.cos(ang_m), jnp.sin(ang_m)
    hm = MLA_ROPE_DIM // 2

    def rope_t(x1, x2):
        return x1 * cos_m - x2 * sin_m, x2 * cos_m + x1 * sin_m

    qt = _dot_nt(wuqt_ref[...], cqn) * (MLA_QK_DIM ** -0.5 * LOG2E)
    for hd in range(MLA_HEADS):
        r0 = hd * MLA_QK_PAD
        qt_ref[0, hd, 0:MLA_NOPE_DIM, :] = qt[r0:r0 + MLA_NOPE_DIM].astype(BF16)
        p0 = r0 + MLA_NOPE_DIM
        n1, n2 = rope_t(qt[p0:p0 + hm], qt[p0 + hm:p0 + 2 * hm])
        qt_ref[0, hd, MLA_NOPE_DIM:MLA_NOPE_DIM + hm, :] = n1.astype(BF16)
        qt_ref[0, hd, MLA_NOPE_DIM + hm:MLA_QK_DIM, :] = n2.astype(BF16)
        qt_ref[0, hd, MLA_QK_DIM:MLA_QK_PAD, :] = jnp.zeros((MLA_QK_PAD - MLA_QK_DIM, tm), BF16)

    kr_t = h_b[:, q_rank + kv_rank:].T
    k1, k2 = rope_t(kr_t[0:hm], kr_t[hm:2 * hm])
    kpe = jnp.concatenate([k1, k2, kr_t[2 * hm:]], axis=0).T.astype(BF16)
    kn = _dot(ckvn, wuk_ref[...])
    for hd in range(MLA_HEADS):
        kf_ref[0, hd, :, 0:MLA_NOPE_DIM] = kn[:, hd * MLA_NOPE_DIM:(hd + 1) * MLA_NOPE_DIM].astype(BF16)
        kf_ref[0, hd, :, MLA_NOPE_DIM:MLA_QK_PAD] = kpe
    vt = _dot_nt(wuvt_ref[...], ckvn)
    for hd in range(MLA_HEADS):
        vt_ref[0, 0, hd, 0:MLA_V_DIM, :] = vt[hd * MLA_V_DIM:(hd + 1) * MLA_V_DIM].astype(BF16)
        vt_ref[0, 0, hd, MLA_V_DIM:MLA_VT_ROWS, :] = jnp.ones((MLA_VT_ROWS - MLA_V_DIM, tm), BF16)

    h_k = in_proj(swa_w, 2 * swa_w)
    ang_a = invfa_ref[...] * pos
    cos_a, sin_a = jnp.cos(ang_a), jnp.sin(ang_a)
    half = SWA_ROT_DIM // 2
    rest = SWA_HEAD_DIM - SWA_ROT_DIM
    ones = jnp.ones((rest, tm), F32)
    zeros = jnp.zeros((rest, tm), F32)
    zhalf = jnp.zeros((half, tm), F32)
    reps = LANES // SWA_HEAD_DIM
    cos_t = jnp.concatenate([cos_a, cos_a, ones] * reps, axis=0).T
    sin_lo = jnp.concatenate([-sin_a, zhalf, zeros] * reps, axis=0).T
    sin_hi = jnp.concatenate([zhalf, sin_a, zeros] * reps, axis=0).T

    def rope_a(xs):
        return (xs * cos_t + pltpu.roll(xs, LANES - half, 1) * sin_lo + pltpu.roll(xs, half, 1) * sin_hi)

    qscale = SWA_HEAD_DIM ** -0.5 * LOG2E
    for j in range(swa_w // LANES):
        c0 = j * LANES
        qa_ref[0, :, c0:c0 + LANES] = (rope_a(h_q[:, c0:c0 + LANES]) * qscale).astype(BF16)
    h_v = in_proj(2 * swa_w, 3 * swa_w)
    for j in range(swa_w // LANES):
        c0 = j * LANES
        ka_ref[0, :, c0:c0 + LANES] = rope_a(h_k[:, c0:c0 + LANES]).astype(BF16)
    va_ref[0] = h_v.astype(BF16)


def _proj_call(x, pos_row, invf_a, invf_m, ln_g, ln_b, w_in_p, qn_g, kvn_g, wuqt, wuk, wuvt, *, tm):
    B, S, D = x.shape
    swa_w = 512
    q_rank = qn_g.shape[-1]
    kv_rank = kvn_g.shape[-1]
    nt = S // tm
    const = lambda *shape: pl.BlockSpec(shape, lambda b, i: (0,) * len(shape))
    out_shape = (
        jax.ShapeDtypeStruct((B, S, swa_w), BF16),
        jax.ShapeDtypeStruct((B, S, swa_w), BF16),
        jax.ShapeDtypeStruct((B, S, swa_w), BF16),
        jax.ShapeDtypeStruct((B, MLA_HEADS, MLA_QK_PAD, S), BF16),
        jax.ShapeDtypeStruct((B, MLA_HEADS, S, MLA_QK_PAD), BF16),
        jax.ShapeDtypeStruct((B, nt, MLA_HEADS, MLA_VT_ROWS, tm), BF16),
    )
    return pl.pallas_call(
        functools.partial(_proj_kernel, swa_w=swa_w, q_rank=q_rank, kv_rank=kv_rank),
        out_shape=out_shape,
        grid=(B, nt),
        in_specs=[
            pl.BlockSpec((1, tm, D), lambda b, i: (b, i, 0)),
            pl.BlockSpec((1, 1, tm), lambda b, i: (b, 0, i)),
            const(*invf_a.shape), const(*invf_m.shape), const(*ln_g.shape), const(*ln_b.shape),
            const(*w_in_p.shape), const(*qn_g.shape), const(*kvn_g.shape),
            const(*wuqt.shape), const(*wuk.shape), const(*wuvt.shape),
        ],
        out_specs=(
            pl.BlockSpec((1, tm, swa_w), lambda b, i: (b, i, 0)),
            pl.BlockSpec((1, tm, swa_w), lambda b, i: (b, i, 0)),
            pl.BlockSpec((1, tm, swa_w), lambda b, i: (b, i, 0)),
            pl.BlockSpec((1, MLA_HEADS, MLA_QK_PAD, tm), lambda b, i: (b, 0, 0, i)),
            pl.BlockSpec((1, MLA_HEADS, tm, MLA_QK_PAD), lambda b, i: (b, 0, i, 0)),
            pl.BlockSpec((1, 1, MLA_HEADS, MLA_VT_ROWS, tm), lambda b, i: (b, i, 0, 0, 0)),
        ),
        compiler_params=pltpu.CompilerParams(
            dimension_semantics=("parallel", "parallel"), vmem_limit_bytes=VMEM_LIMIT),
        name="proj",
    )(x, pos_row, invf_a, invf_m, ln_g, ln_b, w_in_p, qn_g, kvn_g, wuqt, wuk, wuvt)


def _swa_geometry(S, window, d, bq_max):
    L = S // d
    n_side = window // (2 * d)
    bq = min(bq_max, L)
    win = min(L, bq + 2 * n_side)
    return L, n_side, bq, win, L // bq


def _swa_kernel(q_ref, k_ref, v_ref, o_ref, stage, staged, qd, kd, vd, bias, op, mp, dp, ot, mt, dt,
                *, S, bq_max):
    lane = lax.broadcasted_iota(jnp.int32, (1, LANES), 1)
    head0 = lane < SWA_HEAD_DIM
    dils = [d for _, d in SWA_PATTERNS if d != 1]
    assert all(b % a == 0 for a, b in zip([1] + dils, dils)), "each dilation is regrouped from the previous one"
    assert len(dils) <= 2, "the way back to token order uses one intermediate buffer per array"

    def interleave(main, side):
        for n, item in enumerate(main):
            item()
            for extra in side[n * len(side) // len(main):(n + 1) * len(side) // len(main)]:
                extra()

    def regroup_items(level):
        items = []
        d, d_prev = dils[level], ([1] + dils)[level]
        f, L, Lp = d // d_prev, S // d, S // d_prev
        for t, (src, dst) in enumerate(((q_ref, qd), (k_ref, kd), (v_ref, vd))):
            if level == 0:
                nchunk = 4
                for c in range(nchunk):
                    rows = slice(c * S // nchunk, (c + 1) * S // nchunk)

                    def copy(src=src, rows=rows):
                        stage[rows, :] = src[0, rows, :].astype(F32)
                    items.append(copy)
            for rp in range(d_prev):
                for q in range(f):
                    def piece(t=t, dst=dst, r=q * d_prev + rp, start=rp * Lp + q):
                        x = (stage if level == 0 else staged.at[t])[pl.ds(start, L, stride=f), :]
                        dst[level, r * L:(r + 1) * L, :] = x.astype(BF16)
                        if level + 1 < len(dils):
                            staged[t, r * L:(r + 1) * L, :] = x
                    items.append(piece)
        return items

    for p, (window, d) in enumerate(SWA_PATTERNS):
        L, n_side, bq, win, nb = _swa_geometry(S, window, d, bq_max)
        rel = (lax.broadcasted_iota(jnp.int32, (bq, win), 0) - lax.broadcasted_iota(jnp.int32, (bq, win), 1))
        ws_last = min(max((nb - 1) * bq - n_side, 0), L - win)
        for c, delta in enumerate((0, n_side, (nb - 1) * bq - ws_last)):
            bias[p, c, 0:bq, 0:win] = jnp.where(jnp.abs(rel + delta) <= n_side, 0.0, NEG_INF)

    def block_items(p):
        window, d = SWA_PATTERNS[p]
        L, n_side, bq, win, nb = _swa_geometry(S, window, d, bq_max)
        if d == 1:
            load = [lambda rows, ref=ref: ref[0, rows, :] for ref in (q_ref, k_ref, v_ref)]
        else:
            load = [lambda rows, ref=ref: ref[dils.index(d), rows, :] for ref in (qd, kd, vd)]
        items = []
        for n in range(S // bq):
            def block(r=n // nb, i=n % nb):
                row0 = r * L + i * bq
                ws = min(max(i * bq - n_side, 0), L - win)
                q = load[0](slice(row0, row0 + bq))
                k = load[1](slice(r * L + ws, r * L + ws + win))
                v1 = jnp.concatenate([load[2](slice(r * L + ws, r * L + ws + win)),
                                      jnp.ones((win, LANES), BF16)], axis=1)
                mask = bias[p, 0 if i == 0 else (2 if i == nb - 1 else 1), 0:bq, 0:win]
                oas, ms = [], []
                for hsel in (head0, jnp.logical_not(head0)):
                    qh = jnp.where(hsel, q, jnp.zeros_like(q))
                    s = _dot_nt(qh, k) + mask
                    m = jnp.max(s, axis=1, keepdims=True)
                    e = jnp.exp2(s - m).astype(BF16)
                    oas.append(_dot(e, v1))
                    ms.append(jnp.broadcast_to(m, (bq, LANES)))
                op[p, row0:row0 + bq, :] = jnp.where(head0, oas[0][:, :LANES], oas[1][:, :LANES])
                mp[p, row0:row0 + bq, :] = jnp.where(head0, ms[0], ms[1])
                dp[p, row0:row0 + bq, :] = jnp.where(head0, oas[0][:, LANES:], oas[1][:, LANES:])
            items.append(block)
        return items

    def back_items(level):
        p = [d for _, d in SWA_PATTERNS].index(dils[level])
        chain = [1] + dils[:level + 1]
        items = []
        for t, (src, out) in enumerate(((op, ot), (mp, mt), (dp, dt))):
            for step in range(level, -1, -1):
                d, d_prev = chain[step + 1], chain[step]
                f, L, Lp = d // d_prev, S // d, S // d_prev
                for rp in range(d_prev):
                    for q in range(f):
                        def piece(t=t, src=src, out=out, step=step, r=q * d_prev + rp, start=rp * Lp + q,
                                  f=f, L=L):
                            x = src[p, r * L:(r + 1) * L, :] if step == level else staged[t, r * L:(r + 1) * L, :]
                            if step == 0:
                                out[level, pl.ds(start, L, stride=f), :] = x
                            else:
                                staged[t, pl.ds(start, L, stride=f), :] = x
                        items.append(piece)
        return items

    cr = min(512, S)

    def combine_items():
        items = []
        for c in range(S // cr):
            def combine(rows=slice(c * cr, (c + 1) * cr)):
                os_, ms_, ds_ = [], [], []
                for p, (_, d) in enumerate(SWA_PATTERNS):
                    srcs = (op, mp, dp) if d == 1 else (ot, mt, dt)
                    lead = p if d == 1 else dils.index(d)
                    for acc, src in zip((os_, ms_, ds_), srcs):
                        acc.append(src[lead, rows, :])
                m = functools.reduce(jnp.maximum, ms_)
                es = [jnp.exp2(mi - m) for mi in ms_]
                num = functools.reduce(lambda a, b: a + b, [e * o for e, o in zip(es, os_)])
                den = functools.reduce(lambda a, b: a + b, [e * dd for e, dd in zip(es, ds_)])
                o_ref[0, rows, :] = (num / den).astype(o_ref.dtype)
            items.append(combine)
        return items

    order = sorted(range(len(SWA_PATTERNS)), key=lambda p: SWA_PATTERNS[p][1])
    assert SWA_PATTERNS[order[0]][1] == 1, "the undilated pattern needs no regrouping and goes first"
    for n, p in enumerate(order):
        side = []
        if n >= 2:
            side += back_items(n - 2)
        if n < len(dils):
            side += regroup_items(n)
        interleave(block_items(p), side)
    interleave(back_items(len(dils) - 1), [])
    interleave(combine_items(), [])


def _swa_call(q_a, k_a, v_a):
    B, S, W = q_a.shape
    npat = len(SWA_PATTERNS)
    nd = sum(1 for _, d in SWA_PATTERNS if d != 1)
    bq_max = 128
    geo = [_swa_geometry(S, w, d, bq_max) for w, d in SWA_PATTERNS]
    bq, win = max(g[2] for g in geo), max(g[3] for g in geo)
    spec = pl.BlockSpec((1, S, LANES), lambda b, j: (b, 0, j))
    return pl.pallas_call(
        functools.partial(_swa_kernel, S=S, bq_max=bq_max),
        out_shape=jax.ShapeDtypeStruct((B, S, W), BF16),
        grid=(B, W // LANES),
        in_specs=[spec, spec, spec],
        out_specs=spec,
        scratch_shapes=[
            pltpu.VMEM((S, LANES), F32),
            pltpu.VMEM((3, S, LANES), F32),
            pltpu.VMEM((nd, S, LANES), BF16),
            pltpu.VMEM((nd, S, LANES), BF16),
            pltpu.VMEM((nd, S, LANES), BF16),
            pltpu.VMEM((npat, 3, bq, win), F32),
            pltpu.VMEM((npat, S, LANES), F32),
            pltpu.VMEM((npat, S, LANES), F32),
            pltpu.VMEM((npat, S, LANES), F32),
            pltpu.VMEM((nd, S, LANES), F32),
            pltpu.VMEM((nd, S, LANES), F32),
            pltpu.VMEM((nd, S, LANES), F32),
        ],
        compiler_params=pltpu.CompilerParams(
            dimension_semantics=("parallel", "parallel"), vmem_limit_bytes=VMEM_LIMIT),
        name="swa",
    )(q_a, k_a, v_a)


def _mla_kernel(qt_ref, k_ref, vt_ref, o_ref, acc_ref, *, width):
    ngroups = qt_ref.shape[3] // width
    nk, tk = vt_ref.shape[1], vt_ref.shape[4]

    def scores(j, qt):
        return _dot(k_ref[0, 0, j * tk:(j + 1) * tk, :], qt)

    def pv(j, e):
        return _dot(vt_ref[0, j, 0], e.astype(BF16))

    def finish(c):
        acc = acc_ref[c]
        o_ref[0, c * width:(c + 1) * width, :] = (
            acc[:MLA_V_DIM] / acc[MLA_V_DIM:MLA_V_DIM + 1]).T.astype(o_ref.dtype)

    jump = jnp.zeros((1, width), F32)
    for c in range(ngroups):
        qt = qt_ref[0, 0, :, c * width:(c + 1) * width]
        st = scores(0, qt)
        m = jnp.max(st[0:MLA_FIRST_ROWS], axis=0, keepdims=True)
        for j in range(nk):
            st_next = scores(j + 1, qt) if j + 1 < nk else None
            mt = jnp.max(st, axis=0, keepdims=True)
            acc = pv(j, jnp.exp2(st - m))
            if j > 0:
                acc = acc_ref[c] + acc
            jump = jnp.maximum(jump, mt - m)
            m_new = jnp.maximum(m, mt)
            acc_ref[c] = acc * jnp.exp2(m - m_new)
            st, m = st_next, m_new
        finish(c)

    @pl.when(jnp.max(jump) > MLA_JUMP_LIMIT)
    def _():
        for c in range(ngroups):
            qt = qt_ref[0, 0, :, c * width:(c + 1) * width]
            acc_ref[c] = jnp.zeros(acc_ref.shape[1:], F32)
            st = scores(0, qt)
            m = jnp.full((1, width), NEG_INF, F32)
            for j in range(nk):
                st_next = scores(j + 1, qt) if j + 1 < nk else None
                m_new = jnp.maximum(m, jnp.max(st, axis=0, keepdims=True))
                acc_ref[c] = jnp.exp2(m - m_new) * acc_ref[c] + pv(j, jnp.exp2(st - m_new))
                st, m = st_next, m_new
            finish(c)


def _mla_call(qt, kf, vt, *, tq, width):
    B, H, QP, S = qt.shape
    nk, tk = vt.shape[1], vt.shape[4]
    return pl.pallas_call(
        functools.partial(_mla_kernel, width=width),
        out_shape=jax.ShapeDtypeStruct((B, S, H * MLA_V_DIM), BF16),
        grid=(B, H, S // tq),
        in_specs=[
            pl.BlockSpec((1, 1, QP, tq), lambda b, h, i: (b, h, 0, i)),
            pl.BlockSpec((1, 1, S, QP), lambda b, h, i: (b, h, 0, 0)),
            pl.BlockSpec((1, nk, 1, MLA_VT_ROWS, tk), lambda b, h, i: (b, 0, h, 0, 0)),
        ],
        out_specs=pl.BlockSpec((1, tq, MLA_V_DIM), lambda b, h, i: (b, i, h)),
        scratch_shapes=[pltpu.VMEM((tq // width, MLA_VT_ROWS, width), F32)],
        compiler_params=pltpu.CompilerParams(
            dimension_semantics=("parallel", "parallel", "parallel"), vmem_limit_bytes=VMEM_LIMIT),
        name="mla",
    )(qt, kf, vt)


def _ffn_kernel(x_ref, xp_ref, xn_ref, oa_ref, oap_ref, oan_ref, ob_ref, obp_ref, obn_ref,
                lneg_ref, lneb_ref, ong_ref, wo_ref, ln1g_ref, ln1b_ref, wup_ref, cw_ref, cb_ref,
                wdn_ref, ln2g_ref, ln2b_ref, out_ref, st_all, *, alpha, d_ff, n_chunks, tm):
    n_tiles = x_ref.shape[1] // tm
    rows = tm + 2 * HALO
    G = rows // HALO
    i = pl.program_id(1)
    last = pl.num_programs(1) - 1
    nslab = x_ref.shape[2] // LANES
    ck = d_ff // n_chunks

    def head(t):
        r0, r1 = t * tm, (t + 1) * tm
        first_tile, last_tile = t == 0, t == n_tiles - 1

        def with_halo(main, prev_block, next_block, width):
            prev = (prev_block[0].astype(F32)[width - HALO:] if first_tile
                    else main[0, r0 - width:r0, :].astype(F32)[width - HALO:])
            nxt = (next_block[0].astype(F32)[:HALO] if last_tile
                   else main[0, r1:r1 + width, :].astype(F32)[:HALO])
            return jnp.concatenate([prev, main[0, r0:r1, :].astype(F32), nxt], axis=0)

        x = with_halo(x_ref, xp_ref, xn_ref, HALO)
        oa = with_halo(oa_ref, oap_ref, oan_ref, BF16_ROWS)
        ob = with_halo(ob_ref, obp_ref, obn_ref, BF16_ROWS)
        wa = oa.shape[1]
        xn = _layer_norm(x, lneg_ref[...], lneb_ref[...])
        o = jnp.concatenate([_rms_norm(oa, ong_ref[:, :wa]), _rms_norm(ob, ong_ref[:, wa:])], axis=1)
        x1 = _layer_norm(alpha * xn + _dot(o.astype(BF16), wo_ref[...]), ln1g_ref[...], ln1b_ref[...])

        ridx = lax.broadcasted_iota(jnp.int32, (rows, 1), 0)
        inside = jnp.logical_and(jnp.logical_or(ridx >= HALO, jnp.logical_or(i > 0, not first_tile)),
                                 jnp.logical_or(ridx < tm + HALO, jnp.logical_or(i < last, not last_tile)))
        x1m = jnp.where(inside, x1, 0.0)

        st = st_all.at[t]
        for j in range(nslab):
            st[j] = x1m[:, j * LANES:(j + 1) * LANES]
        x1p = jnp.concatenate(
            [jnp.concatenate([st[j, pl.ds(g, HALO, stride=G), :] for j in range(nslab)], axis=1)
             for g in range(G)], axis=0).astype(BF16)
        return x1[HALO:HALO + tm], x1p

    def conv(u, base):
        before = jnp.concatenate([pltpu.roll(u[rows - HALO:rows], 1, 0), u[0:rows - HALO]], axis=0)
        after = jnp.concatenate([u[HALO:rows], pltpu.roll(u[0:HALO], HALO - 1, 0)], axis=0)
        cw = cw_ref[:, base:base + ck]
        return cb_ref[:, base:base + ck] + before * cw[0:1] + u * cw[1:2] + after * cw[2:3]

    def body(x1p, emit_between):
        def up(base):
            return _dot(x1p, wup_ref[:, base:base + ck])

        bases = [(c * ck, d_ff + c * ck) for c in range(n_chunks)]
        u_gate, u_val = up(bases[0][0]), up(bases[0][1])
        between = emit_between()
        y = None
        for c in range(n_chunks):
            more = c + 1 < n_chunks
            gate = conv(u_gate, bases[c][0])
            u_gate = up(bases[c + 1][0]) if more else None
            val = conv(u_val, bases[c][1])
            g = (gate / (1.0 + jnp.exp2(gate * (-LOG2E))) * val).astype(BF16)
            u_val = up(bases[c + 1][1]) if more else None
            yc = _dot(g, wdn_ref[c * ck:(c + 1) * ck, :])
            y = yc if y is None else y + yc
        return y, between

    def tail(t, x1_main, y):
        st = st_all.at[t]
        for g in range(G):
            for j in range(nslab):
                st[j, pl.ds(g, HALO, stride=G), :] = y[g * HALO:(g + 1) * HALO, j * LANES:(j + 1) * LANES]
        y = jnp.concatenate([st[j, HALO:HALO + tm, :] for j in range(nslab)], axis=1)
        out_ref[0, t * tm:(t + 1) * tm, :] = _layer_norm(alpha * x1_main + y, ln2g_ref[...], ln2b_ref[...])

    x1_main, x1p = head(0)
    pending = None
    for t in range(n_tiles):
        def emit_between(t=t, pending=pending):
            if pending is not None:
                tail(*pending)
            return head(t + 1) if t + 1 < n_tiles else None
        y, nxt = body(x1p, emit_between)
        pending = (t, x1_main, y)
        if nxt is not None:
            x1_main, x1p = nxt
    tail(*pending)


def _ffn_call(x, o_a, o_b, lne_g, lne_b, on_g, w_o, ln1_g, ln1_b, w_up, conv_w, conv_b, w_down,
              ln2_g, ln2_b, *, tm, tiles_per_step, alpha, n_chunks):
    B, S, D = x.shape
    d_ff = w_down.shape[0]
    wa, wb = o_a.shape[-1], o_b.shape[-1]
    bm = tm * tiles_per_step
    nt = S // bm
    r8, r16 = bm // HALO, bm // BF16_ROWS
    const = lambda *shape: pl.BlockSpec(shape, lambda b, i: (0,) * len(shape), pipeline_mode=pl.Buffered(1))

    def main(w):
        return pl.BlockSpec((1, bm, w), lambda b, i: (b, i, 0))

    def prev(w, rb, r):
        return pl.BlockSpec((1, rb, w), lambda b, i: (b, jnp.maximum(i * r - 1, 0), 0))

    def nxt(w, rb, r):
        return pl.BlockSpec((1, rb, w), lambda b, i: (b, jnp.minimum((i + 1) * r, S // rb - 1), 0))

    return pl.pallas_call(
        functools.partial(_ffn_kernel, alpha=alpha, d_ff=d_ff, n_chunks=n_chunks, tm=tm),
        out_shape=jax.ShapeDtypeStruct((B, S, D), F32),
        grid=(B, nt),
        in_specs=[
            main(D), prev(D, HALO, r8), nxt(D, HALO, r8),
            main(wa), prev(wa, BF16_ROWS, r16), nxt(wa, BF16_ROWS, r16),
            main(wb), prev(wb, BF16_ROWS, r16), nxt(wb, BF16_ROWS, r16),
            const(*lne_g.shape), const(*lne_b.shape), const(*on_g.shape), const(*w_o.shape),
            const(*ln1_g.shape), const(*ln1_b.shape), const(*w_up.shape), const(*conv_w.shape),
            const(*conv_b.shape), const(*w_down.shape), const(*ln2_g.shape), const(*ln2_b.shape),
        ],
        out_specs=pl.BlockSpec((1, bm, D), lambda b, i: (b, i, 0)),
        scratch_shapes=[pltpu.VMEM((tiles_per_step, D // LANES, tm + 2 * HALO, LANES), F32)],
        compiler_params=pltpu.CompilerParams(
            dimension_semantics=("parallel", "parallel"), vmem_limit_bytes=FFN_VMEM_LIMIT),
        name="ffn",
    )(x, x, x, o_a, o_a, o_a, o_b, o_b, o_b, lne_g, lne_b, on_g, w_o, ln1_g, ln1_b, w_up, conv_w, conv_b,
      w_down, ln2_g, ln2_b)


def _layer(x, positions, ln_emb_g, ln_emb_b, w_in, q_norm_g, w_uq, kv_norm_g, w_ukv, out_norm_g, w_o,
           ln1_g, ln1_b, w_up, conv_w, conv_b, w_down, ln2_g, ln2_b, *, alpha, tm_proj, tq, tm_ffn, n_chunks):
    B, S, D = x.shape
    row = lambda v: v.reshape(1, -1).astype(F32)
    w_in_p = jnp.concatenate([w_in, jnp.zeros((D, LANES - MLA_ROPE_DIM), w_in.dtype)], axis=1).astype(BF16)
    q_rank = w_uq.shape[0]
    wq = w_uq.reshape(q_rank, MLA_HEADS, MLA_QK_DIM)
    wq = jnp.pad(wq, ((0, 0), (0, 0), (0, MLA_QK_PAD - MLA_QK_DIM)))
    wuqt = wq.reshape(q_rank, MLA_HEADS * MLA_QK_PAD).T.astype(BF16)
    kv_rank = w_ukv.shape[0]
    wkv = w_ukv.reshape(kv_rank, MLA_HEADS, MLA_NOPE_DIM + MLA_V_DIM)
    wuk = wkv[:, :, :MLA_NOPE_DIM].reshape(kv_rank, MLA_HEADS * MLA_NOPE_DIM).astype(BF16)
    wuvt = wkv[:, :, MLA_NOPE_DIM:].reshape(kv_rank, MLA_HEADS * MLA_V_DIM).T.astype(BF16)

    def inv_freq(rot_dim):
        half = rot_dim // 2
        return jnp.power(jnp.float32(ROPE_THETA),
                         -jnp.arange(half, dtype=F32) * (2.0 / rot_dim)).reshape(half, 1)

    pos_row = positions.astype(F32).reshape(B, 1, S)
    q_a, k_a, v_a, qt, kf, vt = _proj_call(
        x, pos_row, inv_freq(SWA_ROT_DIM), inv_freq(MLA_ROPE_DIM), row(ln_emb_g), row(ln_emb_b), w_in_p,
        row(q_norm_g), row(kv_norm_g), wuqt, wuk, wuvt, tm=tm_proj)
    o_a = _swa_call(q_a, k_a, v_a)
    o_b = _mla_call(qt, kf, vt, tq=min(4 * tq, S), width=tq)
    return _ffn_call(x, o_a, o_b, row(ln_emb_g), row(ln_emb_b), row(out_norm_g), w_o.astype(BF16),
                     row(ln1_g), row(ln1_b), w_up.astype(BF16), conv_w.astype(F32), row(conv_b),
                     w_down.astype(BF16), row(ln2_g), row(ln2_b), tm=tm_ffn,
                     tiles_per_step=2 if S % (2 * tm_ffn) == 0 else 1, alpha=alpha, n_chunks=n_chunks)


def kernel(x, positions, ln_emb_g, ln_emb_b, w_in, q_norm_g, w_uq, kv_norm_g, w_ukv, out_norm_g, w_o,
           ln1_g, ln1_b, w_up, conv_w, conv_b, w_down, ln2_g, ln2_b):
    depth = w_in.shape[0]
    assert depth == 1, "single-layer stack"
    alpha = (2.0 * depth) ** 0.25
    S = x.shape[1]
    tile = min(512, S)
    return _layer(x, positions, ln_emb_g, ln_emb_b, w_in[0], q_norm_g[0], w_uq[0], kv_norm_g[0], w_ukv[0],
                  out_norm_g[0], w_o[0], ln1_g[0], ln1_b[0], w_up[0], conv_w[0], conv_b[0], w_down[0],
                  ln2_g[0], ln2_b[0], alpha=alpha, tm_proj=tile, tq=tile, tm_ffn=tile, n_chunks=2)
```

```python
import functools
import math

import jax
import jax.numpy as jnp
from jax import lax
from jax.experimental import pallas as pl
from jax.experimental.pallas import tpu as pltpu

F32 = jnp.float32
BF16 = jnp.bfloat16

SWA_HEAD_DIM = 64
SWA_PATTERNS = ((128, 1), (512, 4), (2048, 16))
SWA_ROT_DIM = 16
MLA_NOPE_DIM = 128
MLA_ROPE_DIM = 64
MLA_V_DIM = 128
MLA_QK_DIM = MLA_NOPE_DIM + MLA_ROPE_DIM
MLA_HEADS = 4
MLA_QK_PAD = 256
ROPE_THETA = 500000.0
LN_EPS = 1e-5
RMS_EPS = 1e-6
NEG_INF = -1e30
CONV_WIDTH = 3
LOG2E = math.log2(math.e)

LANES = 128
MXU_WIDTH = 256
HALO = 8
BF16_ROWS = 16
MLA_VT_ROWS = MLA_V_DIM + BF16_ROWS
MLA_JUMP_LIMIT = 60.0
MLA_FIRST_ROWS = 128
VMEM_LIMIT = 56 * 1024 * 1024
FFN_VMEM_LIMIT = 62 * 1024 * 1024


def _dot(a, b):
    return jnp.dot(a, b, preferred_element_type=F32)


def _dot_nt(a, b):
    return lax.dot_general(a, b, (((1,), (1,)), ((), ())), preferred_element_type=F32)


def _layer_norm(x, g, b):
    mu = jnp.mean(x, axis=-1, keepdims=True)
    xc = x - mu
    var = jnp.mean(xc * xc, axis=-1, keepdims=True)
    return xc * lax.rsqrt(var + LN_EPS) * g + b


def _rms_norm(x, g):
    return x * lax.rsqrt(jnp.mean(x * x, axis=-1, keepdims=True) + RMS_EPS) * g


def _proj_kernel(x_ref, pos_ref, invfa_ref, invfm_ref, lng_ref, lnb_ref, win_ref, qng_ref, kvng_ref,
                 wuqt_ref, wuk_ref, wuvt_ref,
                 qa_ref, ka_ref, va_ref, qt_ref, kf_ref, vt_ref, *, swa_w, q_rank, kv_rank):
    tm = x_ref.shape[1]
    xn = _layer_norm(x_ref[0], lng_ref[...], lnb_ref[...]).astype(BF16)
    o_cq = 3 * swa_w
    o_ckv = o_cq + q_rank
    o_kr = o_ckv + kv_rank

    def in_proj(c0, c1):
        return _dot(xn, win_ref[:, c0:c1])

    h_b = in_proj(o_cq, o_kr + LANES)
    h_q = in_proj(0, swa_w)

    pos = pos_ref[0]
    cqn = _rms_norm(h_b[:, 0:q_rank], qng_ref[...]).astype(BF16)
    ckvn = _rms_norm(h_b[:, q_rank:q_rank + kv_rank], kvng_ref[...]).astype(BF16)
    ang_m = invfm_ref[...] * pos
    cos_m, sin_m = jnp.cos(ang_m), jnp.sin(ang_m)
    hm = MLA_ROPE_DIM // 2

    def rope_t(x1, x2):
        return x1 * cos_m - x2 * sin_m, x2 * cos_m + x1 * sin_m

    qt = _dot_nt(wuqt_ref[...], cqn) * (MLA_QK_DIM ** -0.5 * LOG2E)
    for hd in range(MLA_HEADS):
        r0 = hd * MLA_QK_DIM
        qt_ref[0, hd, 0:MLA_NOPE_DIM, :] = qt[r0:r0 + MLA_NOPE_DIM].astype(BF16)
        p0 = r0 + MLA_NOPE_DIM
        n1, n2 = rope_t(qt[p0:p0 + hm], qt[p0 + hm:p0 + 2 * hm])
        qt_ref[0, hd, MLA_NOPE_DIM:MLA_NOPE_DIM + hm, :] = n1.astype(BF16)
        qt_ref[0, hd, MLA_NOPE_DIM + hm:MLA_QK_DIM, :] = n2.astype(BF16)
        qt_ref[0, hd, MLA_QK_DIM:MLA_QK_PAD, :] = jnp.zeros((MLA_QK_PAD - MLA_QK_DIM, tm), BF16)

    kr_t = h_b[:, q_rank + kv_rank:].T
    k1, k2 = rope_t(kr_t[0:hm], kr_t[hm:2 * hm])
    kpe = jnp.concatenate([k1, k2, kr_t[2 * hm:]], axis=0).T.astype(BF16)
    kn = _dot(ckvn, wuk_ref[...])
    for hd in range(MLA_HEADS):
        kf_ref[0, hd, :, 0:MLA_NOPE_DIM] = kn[:, hd * MLA_NOPE_DIM:(hd + 1) * MLA_NOPE_DIM].astype(BF16)
        kf_ref[0, hd, :, MLA_NOPE_DIM:MLA_QK_PAD] = kpe
    vt = _dot_nt(wuvt_ref[...], ckvn)
    for hd in range(MLA_HEADS):
        vt_ref[0, 0, hd, 0:MLA_V_DIM, :] = vt[hd * MLA_V_DIM:(hd + 1) * MLA_V_DIM].astype(BF16)
        vt_ref[0, 0, hd, MLA_V_DIM:MLA_VT_ROWS, :] = jnp.ones((MLA_VT_ROWS - MLA_V_DIM, tm), BF16)

    h_k = in_proj(swa_w, 2 * swa_w)
    ang_a = invfa_ref[...] * pos
    cos_a, sin_a = jnp.cos(ang_a), jnp.sin(ang_a)
    half = SWA_ROT_DIM // 2
    rest = SWA_HEAD_DIM - SWA_ROT_DIM
    ones = jnp.ones((rest, tm), F32)
    zeros = jnp.zeros((rest, tm), F32)
    zhalf = jnp.zeros((half, tm), F32)
    reps = LANES // SWA_HEAD_DIM
    cos_t = jnp.concatenate([cos_a, cos_a, ones] * reps, axis=0).T
    sin_lo = jnp.concatenate([-sin_a, zhalf, zeros] * reps, axis=0).T
    sin_hi = jnp.concatenate([zhalf, sin_a, zeros] * reps, axis=0).T

    def rope_a(xs):
        return (xs * cos_t + pltpu.roll(xs, LANES - half, 1) * sin_lo + pltpu.roll(xs, half, 1) * sin_hi)

    qscale = SWA_HEAD_DIM ** -0.5 * LOG2E
    for j in range(swa_w // LANES):
        c0 = j * LANES
        qa_ref[0, :, c0:c0 + LANES] = (rope_a(h_q[:, c0:c0 + LANES]) * qscale).astype(BF16)
    h_v = in_proj(2 * swa_w, 3 * swa_w)
    for j in range(swa_w // LANES):
        c0 = j * LANES
        ka_ref[0, :, c0:c0 + LANES] = rope_a(h_k[:, c0:c0 + LANES]).astype(BF16)
    va_ref[0] = h_v.astype(BF16)


def _proj_call(x, pos_row, invf_a, invf_m, ln_g, ln_b, w_in_p, qn_g, kvn_g, wuqt, wuk, wuvt, *, tm):
    B, S, D = x.shape
    swa_w = 512
    q_rank = qn_g.shape[-1]
    kv_rank = kvn_g.shape[-1]
    nt = S // tm
    const = lambda *shape: pl.BlockSpec(shape, lambda b, i: (0,) * len(shape))
    out_shape = (
        jax.ShapeDtypeStruct((B, S, swa_w), BF16),
        jax.ShapeDtypeStruct((B, S, swa_w), BF16),
        jax.ShapeDtypeStruct((B, S, swa_w), BF16),
        jax.ShapeDtypeStruct((B, MLA_HEADS, MLA_QK_PAD, S), BF16),
        jax.ShapeDtypeStruct((B, MLA_HEADS, S, MLA_QK_PAD), BF16),
        jax.ShapeDtypeStruct((B, nt, MLA_HEADS, MLA_VT_ROWS, tm), BF16),
    )
    return pl.pallas_call(
        functools.partial(_proj_kernel, swa_w=swa_w, q_rank=q_rank, kv_rank=kv_rank),
        out_shape=out_shape,
        grid=(B, nt),
        in_specs=[
            pl.BlockSpec((1, tm, D), lambda b, i: (b, i, 0)),
            pl.BlockSpec((1, 1, tm), lambda b, i: (b, 0, i)),
            const(*invf_a.shape), const(*invf_m.shape), const(*ln_g.shape), const(*ln_b.shape),
            const(*w_in_p.shape), const(*qn_g.shape), const(*kvn_g.shape),
            const(*wuqt.shape), const(*wuk.shape), const(*wuvt.shape),
        ],
        out_specs=(
            pl.BlockSpec((1, tm, swa_w), lambda b, i: (b, i, 0)),
            pl.BlockSpec((1, tm, swa_w), lambda b, i: (b, i, 0)),
            pl.BlockSpec((1, tm, swa_w), lambda b, i: (b, i, 0)),
            pl.BlockSpec((1, MLA_HEADS, MLA_QK_PAD, tm), lambda b, i: (b, 0, 0, i)),
            pl.BlockSpec((1, MLA_HEADS, tm, MLA_QK_PAD), lambda b, i: (b, 0, i, 0)),
            pl.BlockSpec((1, 1, MLA_HEADS, MLA_VT_ROWS, tm), lambda b, i: (b, i, 0, 0, 0)),
        ),
        compiler_params=pltpu.CompilerParams(
            dimension_semantics=("parallel", "parallel"), vmem_limit_bytes=VMEM_LIMIT),
        name="proj",
    )(x, pos_row, invf_a, invf_m, ln_g, ln_b, w_in_p, qn_g, kvn_g, wuqt, wuk, wuvt)


def _swa_geometry(S, window, d, bq_max):
    L = S // d
    n_side = window // (2 * d)
    bq = min(bq_max, L)
    win = min(L, bq + 2 * n_side)
    return L, n_side, bq, win, L // bq


def _swa_kernel(q_ref, k_ref, v_ref, o_ref, stage, staged, qd, kd, vd, bias, op, mp, dp, ot, mt, dt,
                *, S, bq_max):
    lane = lax.broadcasted_iota(jnp.int32, (1, LANES), 1)
    head0 = lane < SWA_HEAD_DIM
    dils = [d for _, d in SWA_PATTERNS if d != 1]
    assert all(b % a == 0 for a, b in zip([1] + dils, dils)), "each dilation is regrouped from the previous one"
    assert len(dils) <= 2, "the way back to token order uses one intermediate buffer per array"

    def interleave(main, side):
        for n, item in enumerate(main):
            item()
            for extra in side[n * len(side) // len(main):(n + 1) * len(side) // len(main)]:
                extra()

    def regroup_items(level):
        items = []
        d, d_prev = dils[level], ([1] + dils)[level]
        f, L, Lp = d // d_prev, S // d, S // d_prev
        for t, (src, dst) in enumerate(((q_ref, qd), (k_ref, kd), (v_ref, vd))):
            if level == 0:
                nchunk = 4
                for c in range(nchunk):
                    rows = slice(c * S // nchunk, (c + 1) * S // nchunk)

                    def copy(src=src, rows=rows):
                        stage[rows, :] = src[0, rows, :].astype(F32)
                    items.append(copy)
            for rp in range(d_prev):
                for q in range(f):
                    def piece(t=t, dst=dst, r=q * d_prev + rp, start=rp * Lp + q):
                        x = (stage if level == 0 else staged.at[t])[pl.ds(start, L, stride=f), :]
                        dst[level, r * L:(r + 1) * L, :] = x.astype(BF16)
                        if level + 1 < len(dils):
                            staged[t, r * L:(r + 1) * L, :] = x
                    items.append(piece)
        return items

    for p, (window, d) in enumerate(SWA_PATTERNS):
        L, n_side, bq, win, nb = _swa_geometry(S, window, d, bq_max)
        rel = (lax.broadcasted_iota(jnp.int32, (bq, win), 0) - lax.broadcasted_iota(jnp.int32, (bq, win), 1))
        ws_last = min(max((nb - 1) * bq - n_side, 0), L - win)
        for c, delta in enumerate((0, n_side, (nb - 1) * bq - ws_last)):
            bias[p, c, 0:bq, 0:win] = jnp.where(jnp.abs(rel + delta) <= n_side, 0.0, NEG_INF)

    def block_items(p):
        window, d = SWA_PATTERNS[p]
        L, n_side, bq, win, nb = _swa_geometry(S, window, d, bq_max)
        if d == 1:
            load = [lambda rows, ref=ref: ref[0, rows, :] for ref in (q_ref, k_ref, v_ref)]
        else:
            load = [lambda rows, ref=ref: ref[dils.index(d), rows, :] for ref in (qd, kd, vd)]
        items = []
        for n in range(S // bq):
            def block(r=n // nb, i=n % nb):
                row0 = r * L + i * bq
                ws = min(max(i * bq - n_side, 0), L - win)
                q = load[0](slice(row0, row0 + bq))
                k = load[1](slice(r * L + ws, r * L + ws + win))
                v1 = jnp.concatenate([load[2](slice(r * L + ws, r * L + ws + win)),
                                      jnp.ones((win, LANES), BF16)], axis=1)
                mask = bias[p, 0 if i == 0 else (2 if i == nb - 1 else 1), 0:bq, 0:win]
                oas, ms = [], []
                for hsel in (head0, jnp.logical_not(head0)):
                    qh = jnp.where(hsel, q, jnp.zeros_like(q))
                    s = _dot_nt(qh, k) + mask
                    m = jnp.max(s, axis=1, keepdims=True)
                    e = jnp.exp2(s - m).astype(BF16)
                    oas.append(_dot(e, v1))
                    ms.append(jnp.broadcast_to(m, (bq, LANES)))
                op[p, row0:row0 + bq, :] = jnp.where(head0, oas[0][:, :LANES], oas[1][:, :LANES])
                mp[p, row0:row0 + bq, :] = jnp.where(head0, ms[0], ms[1])
                dp[p, row0:row0 + bq, :] = jnp.where(head0, oas[0][:, LANES:], oas[1][:, LANES:])
            items.append(block)
        return items

    def back_items(level):
        p = [d for _, d in SWA_PATTERNS].index(dils[level])
        chain = [1] + dils[:level + 1]
        items = []
        for t, (src, out) in enumerate(((op, ot), (mp, mt), (dp, dt))):
            for step in range(level, -1, -1):
                d, d_prev = chain[step + 1], chain[step]
                f, L, Lp = d // d_prev, S // d, S // d_prev
                for rp in range(d_prev):
                    for q in range(f):
                        def piece(t=t, src=src, out=out, step=step, r=q * d_prev + rp, start=rp * Lp + q,
                                  f=f, L=L):
                            x = src[p, r * L:(r + 1) * L, :] if step == level else staged[t, r * L:(r + 1) * L, :]
                            if step == 0:
                                out[level, pl.ds(start, L, stride=f), :] = x
                            else:
                                staged[t, pl.ds(start, L, stride=f), :] = x
                        items.append(piece)
        return items

    cr = min(512, S)

    def combine_items():
        items = []
        for c in range(S // cr):
            def combine(rows=slice(c * cr, (c + 1) * cr)):
                os_, ms_, ds_ = [], [], []
                for p, (_, d) in enumerate(SWA_PATTERNS):
                    srcs = (op, mp, dp) if d == 1 else (ot, mt, dt)
                    lead = p if d == 1 else dils.index(d)
                    for acc, src in zip((os_, ms_, ds_), srcs):
                        acc.append(src[lead, rows, :])
                m = functools.reduce(jnp.maximum, ms_)
                es = [jnp.exp2(mi - m) for mi in ms_]
                num = functools.reduce(lambda a, b: a + b, [e * o for e, o in zip(es, os_)])
                den = functools.reduce(lambda a, b: a + b, [e * dd for e, dd in zip(es, ds_)])
                o_ref[0, rows, :] = (num / den).astype(o_ref.dtype)
            items.append(combine)
        return items

    order = sorted(range(len(SWA_PATTERNS)), key=lambda p: SWA_PATTERNS[p][1])
    assert SWA_PATTERNS[order[0]][1] == 1, "the undilated pattern needs no regrouping and goes first"
    for n, p in enumerate(order):
        side = []
        if n >= 2:
            side += back_items(n - 2)
        if n < len(dils):
            side += regroup_items(n)
        interleave(block_items(p), side)
    interleave(back_items(len(dils) - 1), [])
    interleave(combine_items(), [])


def _swa_call(q_a, k_a, v_a):
    B, S, W = q_a.shape
    npat = len(SWA_PATTERNS)
    nd = sum(1 for _, d in SWA_PATTERNS if d != 1)
    bq_max = 128
    geo = [_swa_geometry(S, w, d, bq_max) for w, d in SWA_PATTERNS]
    bq, win = max(g[2] for g in geo), max(g[3] for g in geo)
    spec = pl.BlockSpec((1, S, LANES), lambda b, j: (b, 0, j))
    return pl.pallas_call(
        functools.partial(_swa_kernel, S=S, bq_max=bq_max),
        out_shape=jax.ShapeDtypeStruct((B, S, W), BF16),
        grid=(B, W // LANES),
        in_specs=[spec, spec, spec],
        out_specs=spec,
        scratch_shapes=[
            pltpu.VMEM((S, LANES), F32),
            pltpu.VMEM((3, S, LANES), F32),
            pltpu.VMEM((nd, S, LANES), BF16),
            pltpu.VMEM((nd, S, LANES), BF16),
            pltpu.VMEM((nd, S, LANES), BF16),
            pltpu.VMEM((npat, 3, bq, win), F32),
            pltpu.VMEM((npat, S, LANES), F32),
            pltpu.VMEM((npat, S, LANES), F32),
            pltpu.VMEM((npat, S, LANES), F32),
            pltpu.VMEM((nd, S, LANES), F32),
            pltpu.VMEM((nd, S, LANES), F32),
            pltpu.VMEM((nd, S, LANES), F32),
        ],
        compiler_params=pltpu.CompilerParams(
            dimension_semantics=("parallel", "parallel"), vmem_limit_bytes=VMEM_LIMIT),
        name="swa",
    )(q_a, k_a, v_a)


def _mla_kernel(qt_ref, k_ref, vt_ref, o_ref, acc_ref, *, width):
    ngroups = qt_ref.shape[3] // width
    nk, tk = vt_ref.shape[1], vt_ref.shape[4]

    def scores(j, qt):
        return _dot(k_ref[0, 0, j * tk:(j + 1) * tk, :], qt)

    def pv(j, e):
        return _dot(vt_ref[0, j, 0], e.astype(BF16))

    def finish(c):
        acc = acc_ref[c]
        o_ref[0, c * width:(c + 1) * width, :] = (
            acc[:MLA_V_DIM] / acc[MLA_V_DIM:MLA_V_DIM + 1]).T.astype(o_ref.dtype)

    jump = jnp.zeros((1, width), F32)
    for c in range(ngroups):
        qt = qt_ref[0, 0, :, c * width:(c + 1) * width]
        st = scores(0, qt)
        m = jnp.max(st[0:MLA_FIRST_ROWS], axis=0, keepdims=True)
        for j in range(nk):
            st_next = scores(j + 1, qt) if j + 1 < nk else None
            mt = jnp.max(st, axis=0, keepdims=True)
            acc = pv(j, jnp.exp2(st - m))
            if j > 0:
                acc = acc_ref[c] + acc
            jump = jnp.maximum(jump, mt - m)
            m_new = jnp.maximum(m, mt)
            acc_ref[c] = acc * jnp.exp2(m - m_new)
            st, m = st_next, m_new
        finish(c)

    @pl.when(jnp.max(jump) > MLA_JUMP_LIMIT)
    def _():
        for c in range(ngroups):
            qt = qt_ref[0, 0, :, c * width:(c + 1) * width]
            acc_ref[c] = jnp.zeros(acc_ref.shape[1:], F32)
            st = scores(0, qt)
            m = jnp.full((1, width), NEG_INF, F32)
            for j in range(nk):
                st_next = scores(j + 1, qt) if j + 1 < nk else None
                m_new = jnp.maximum(m, jnp.max(st, axis=0, keepdims=True))
                acc_ref[c] = jnp.exp2(m - m_new) * acc_ref[c] + pv(j, jnp.exp2(st - m_new))
                st, m = st_next, m_new
            finish(c)


def _mla_call(qt, kf, vt, *, tq, width):
    B, H, QP, S = qt.shape
    nk, tk = vt.shape[1], vt.shape[4]
    return pl.pallas_call(
        functools.partial(_mla_kernel, width=width),
        out_shape=jax.ShapeDtypeStruct((B, S, H * MLA_V_DIM), BF16),
        grid=(B, H, S // tq),
        in_specs=[
            pl.BlockSpec((1, 1, QP, tq), lambda b, h, i: (b, h, 0, i)),
            pl.BlockSpec((1, 1, S, QP), lambda b, h, i: (b, h, 0, 0)),
            pl.BlockSpec((1, nk, 1, MLA_VT_ROWS, tk), lambda b, h, i: (b, 0, h, 0, 0)),
        ],
        out_specs=pl.BlockSpec((1, tq, MLA_V_DIM), lambda b, h, i: (b, i, h)),
        scratch_shapes=[pltpu.VMEM((tq // width, MLA_VT_ROWS, width), F32)],
        compiler_params=pltpu.CompilerParams(
            dimension_semantics=("parallel", "parallel", "parallel"), vmem_limit_bytes=VMEM_LIMIT),
        name="mla",
    )(qt, kf, vt)


def _ffn_kernel(x_ref, xp_ref, xn_ref, oa_ref, oap_ref, oan_ref, ob_ref, obp_ref, obn_ref,
                lneg_ref, lneb_ref, ong_ref, wo_ref, ln1g_ref, ln1b_ref, wup_ref, cw_ref, cb_ref,
                wdn_ref, ln2g_ref, ln2b_ref, out_ref, st_all, *, alpha, d_ff, n_chunks, tm):
    n_tiles = x_ref.shape[1] // tm
    rows = tm + 2 * HALO
    G = rows // HALO
    i = pl.program_id(1)
    last = pl.num_programs(1) - 1
    nslab = x_ref.shape[2] // LANES
    cuts = [0] + [d_ff * (n + 1) // n_chunks // MXU_WIDTH * MXU_WIDTH for n in range(n_chunks - 1)] + [d_ff]
    chunks = list(zip(cuts[:-1], cuts[1:]))

    def head(t):
        r0, r1 = t * tm, (t + 1) * tm
        first_tile, last_tile = t == 0, t == n_tiles - 1

        def with_halo(main, prev_block, next_block, width):
            prev = (prev_block[0].astype(F32)[width - HALO:] if first_tile
                    else main[0, r0 - width:r0, :].astype(F32)[width - HALO:])
            nxt = (next_block[0].astype(F32)[:HALO] if last_tile
                   else main[0, r1:r1 + width, :].astype(F32)[:HALO])
            return jnp.concatenate([prev, main[0, r0:r1, :].astype(F32), nxt], axis=0)

        x = with_halo(x_ref, xp_ref, xn_ref, HALO)
        oa = with_halo(oa_ref, oap_ref, oan_ref, BF16_ROWS)
        ob = with_halo(ob_ref, obp_ref, obn_ref, BF16_ROWS)
        wa = oa.shape[1]
        xn = _layer_norm(x, lneg_ref[...], lneb_ref[...])
        o = jnp.concatenate([_rms_norm(oa, ong_ref[:, :wa]), _rms_norm(ob, ong_ref[:, wa:])], axis=1)
        x1 = _layer_norm(alpha * xn + _dot(o.astype(BF16), wo_ref[...]), ln1g_ref[...], ln1b_ref[...])

        ridx = lax.broadcasted_iota(jnp.int32, (rows, 1), 0)
        inside = jnp.logical_and(jnp.logical_or(ridx >= HALO, jnp.logical_or(i > 0, not first_tile)),
                                 jnp.logical_or(ridx < tm + HALO, jnp.logical_or(i < last, not last_tile)))
        x1m = jnp.where(inside, x1, 0.0)

        st = st_all.at[t]
        for j in range(nslab):
            st[j] = x1m[:, j * LANES:(j + 1) * LANES]
        x1p = jnp.concatenate(
            [jnp.concatenate([st[j, pl.ds(g, HALO, stride=G), :] for j in range(nslab)], axis=1)
             for g in range(G)], axis=0).astype(BF16)
        return x1[HALO:HALO + tm], x1p

    def conv(u, c0, c1):
        before = jnp.concatenate([pltpu.roll(u[rows - HALO:rows], 1, 0), u[0:rows - HALO]], axis=0)
        after = jnp.concatenate([u[HALO:rows], pltpu.roll(u[0:HALO], HALO - 1, 0)], axis=0)
        cw = cw_ref[:, c0:c1]
        return cb_ref[:, c0:c1] + before * cw[0:1] + u * cw[1:2] + after * cw[2:3]

    def body(x1p, emit_between):
        def up(c0, c1):
            return _dot(x1p, wup_ref[:, c0:c1])

        (c0, c1) = chunks[0]
        u_gate, u_val = up(c0, c1), up(d_ff + c0, d_ff + c1)
        between = emit_between()
        y = None
        for n, (c0, c1) in enumerate(chunks):
            nxt = chunks[n + 1] if n + 1 < len(chunks) else None
            gate = conv(u_gate, c0, c1)
            u_gate = up(*nxt) if nxt else None
            val = conv(u_val, d_ff + c0, d_ff + c1)
            g = (gate / (1.0 + jnp.exp2(gate * (-LOG2E))) * val).astype(BF16)
            u_val = up(d_ff + nxt[0], d_ff + nxt[1]) if nxt else None
            yc = _dot(g, wdn_ref[c0:c1, :])
            y = yc if y is None else y + yc
        return y, between

    def tail(t, x1_main, y):
        st = st_all.at[t]
        for g in range(G):
            for j in range(nslab):
                st[j, pl.ds(g, HALO, stride=G), :] = y[g * HALO:(g + 1) * HALO, j * LANES:(j + 1) * LANES]
        y = jnp.concatenate([st[j, HALO:HALO + tm, :] for j in range(nslab)], axis=1)
        out_ref[0, t * tm:(t + 1) * tm, :] = _layer_norm(alpha * x1_main + y, ln2g_ref[...], ln2b_ref[...])

    x1_main, x1p = head(0)
    pending = None
    for t in range(n_tiles):
        def emit_between(t=t, pending=pending):
            if pending is not None:
                tail(*pending)
            return head(t + 1) if t + 1 < n_tiles else None
        y, nxt = body(x1p, emit_between)
        pending = (t, x1_main, y)
        if nxt is not None:
            x1_main, x1p = nxt
    tail(*pending)


def _ffn_call(x, o_a, o_b, lne_g, lne_b, on_g, w_o, ln1_g, ln1_b, w_up, conv_w, conv_b, w_down,
              ln2_g, ln2_b, *, tm, tiles_per_step, alpha, n_chunks):
    B, S, D = x.shape
    d_ff = w_down.shape[0]
    wa, wb = o_a.shape[-1], o_b.shape[-1]
    bm = tm * tiles_per_step
    nt = S // bm
    r8, r16 = bm // HALO, bm // BF16_ROWS
    const = lambda *shape: pl.BlockSpec(shape, lambda b, i: (0,) * len(shape), pipeline_mode=pl.Buffered(1))

    def main(w):
        return pl.BlockSpec((1, bm, w), lambda b, i: (b, i, 0))

    def prev(w, rb, r):
        return pl.BlockSpec((1, rb, w), lambda b, i: (b, jnp.maximum(i * r - 1, 0), 0))

    def nxt(w, rb, r):
        return pl.BlockSpec((1, rb, w), lambda b, i: (b, jnp.minimum((i + 1) * r, S // rb - 1), 0))

    return pl.pallas_call(
        functools.partial(_ffn_kernel, alpha=alpha, d_ff=d_ff, n_chunks=n_chunks, tm=tm),
        out_shape=jax.ShapeDtypeStruct((B, S, D), F32),
        grid=(B, nt),
        in_specs=[
            main(D), prev(D, HALO, r8), nxt(D, HALO, r8),
            main(wa), prev(wa, BF16_ROWS, r16), nxt(wa, BF16_ROWS, r16),
            main(wb), prev(wb, BF16_ROWS, r16), nxt(wb, BF16_ROWS, r16),
            const(*lne_g.shape), const(*lne_b.shape), const(*on_g.shape), const(*w_o.shape),
            const(*ln1_g.shape), const(*ln1_b.shape), const(*w_up.shape), const(*conv_w.shape),
            const(*conv_b.shape), const(*w_down.shape), const(*ln2_g.shape), const(*ln2_b.shape),
        ],
        out_specs=pl.BlockSpec((1, bm, D), lambda b, i: (b, i, 0)),
        scratch_shapes=[pltpu.VMEM((tiles_per_step, D // LANES, tm + 2 * HALO, LANES), F32)],
        compiler_params=pltpu.CompilerParams(
            dimension_semantics=("parallel", "parallel"), vmem_limit_bytes=FFN_VMEM_LIMIT),
        name="ffn",
    )(x, x, x, o_a, o_a, o_a, o_b, o_b, o_b, lne_g, lne_b, on_g, w_o, ln1_g, ln1_b, w_up, conv_w, conv_b,
      w_down, ln2_g, ln2_b)


def _layer(x, positions, ln_emb_g, ln_emb_b, w_in, q_norm_g, w_uq, kv_norm_g, w_ukv, out_norm_g, w_o,
           ln1_g, ln1_b, w_up, conv_w, conv_b, w_down, ln2_g, ln2_b, *, alpha, tm_proj, tq, tm_ffn, n_chunks):
    B, S, D = x.shape
    row = lambda v: v.reshape(1, -1).astype(F32)
    w_in_p = jnp.concatenate([w_in, jnp.zeros((D, LANES - MLA_ROPE_DIM), w_in.dtype)], axis=1).astype(BF16)
    q_rank = w_uq.shape[0]
    wuqt = w_uq.T.astype(BF16)
    kv_rank = w_ukv.shape[0]
    wkv = w_ukv.reshape(kv_rank, MLA_HEADS, MLA_NOPE_DIM + MLA_V_DIM)
    wuk = wkv[:, :, :MLA_NOPE_DIM].reshape(kv_rank, MLA_HEADS * MLA_NOPE_DIM).astype(BF16)
    wuvt = wkv[:, :, MLA_NOPE_DIM:].reshape(kv_rank, MLA_HEADS * MLA_V_DIM).T.astype(BF16)

    def inv_freq(rot_dim):
        half = rot_dim // 2
        return jnp.power(jnp.float32(ROPE_THETA),
                         -jnp.arange(half, dtype=F32) * (2.0 / rot_dim)).reshape(half, 1)

    pos_row = positions.astype(F32).reshape(B, 1, S)
    q_a, k_a, v_a, qt, kf, vt = _proj_call(
        x, pos_row, inv_freq(SWA_ROT_DIM), inv_freq(MLA_ROPE_DIM), row(ln_emb_g), row(ln_emb_b), w_in_p,
        row(q_norm_g), row(kv_norm_g), wuqt, wuk, wuvt, tm=tm_proj)
    o_a = _swa_call(q_a, k_a, v_a)
    o_b = _mla_call(qt, kf, vt, tq=min(4 * tq, S), width=tq)
    return _ffn_call(x, o_a, o_b, row(ln_emb_g), row(ln_emb_b), row(out_norm_g), w_o.astype(BF16),
                     row(ln1_g), row(ln1_b), w_up.astype(BF16), conv_w.astype(F32), row(conv_b),
                     w_down.astype(BF16), row(ln2_g), row(ln2_b), tm=tm_ffn,
                     tiles_per_step=2 if S % (2 * tm_ffn) == 0 else 1, alpha=alpha, n_chunks=n_chunks)


def kernel(x, positions, ln_emb_g, ln_emb_b, w_in, q_norm_g, w_uq, kv_norm_g, w_ukv, out_norm_g, w_o,
           ln1_g, ln1_b, w_up, conv_w, conv_b, w_down, ln2_g, ln2_b):
    depth = w_in.shape[0]
    assert depth == 1, "single-layer stack"
    alpha = (2.0 * depth) ** 0.25
    S = x.shape[1]
    tile = min(512, S)
    return _layer(x, positions, ln_emb_g, ln_emb_b, w_in[0], q_norm_g[0], w_uq[0], kv_norm_g[0], w_ukv[0],
                  out_norm_g[0], w_o[0], ln1_g[0], ln1_b[0], w_up[0], conv_w[0], conv_b[0], w_down[0],
                  ln2_g[0], ln2_b[0], alpha=alpha, tm_proj=tile, tq=tile, tm_ffn=tile, n_chunks=2)
```
